```python
import math, functools
import jax, jax.numpy as jnp
from jax import lax
import numpy as np

D_MODEL = 1024
BATCH = 32
SEQ = 256
DEPTH = 4
DEC_BATCH = 4
DEC_SEQ = 1024
PAST_LEN = 256

GRID_W = 64
N_HEADS = 8
QK_NOPE = 64
QK_ROPE = 32
V_HEAD = 64
KV_LORA = 256
Q_LORA = 384
MLA_W = N_HEADS * V_HEAD
LRU_W = 512
LRU_BLOCKS = 8
LRU_BD = LRU_W // LRU_BLOCKS
LRU_C = 8.0
CONV_W = 4
CONV_LEFT = 2
D_FF = 2816
IN_W = Q_LORA + KV_LORA + QK_ROPE + 2 * LRU_W
IN_SPLITS = (Q_LORA, Q_LORA + KV_LORA, Q_LORA + KV_LORA + QK_ROPE, Q_LORA + KV_LORA + QK_ROPE + LRU_W)
MIX_W = MLA_W + LRU_W
N_MOD = 9
ALPHA = (2.0 * DEPTH) ** 0.25
BETA = (8.0 * DEPTH) ** -0.25
ROPE_BASE = 10000.0
ATTN_SCALE = 1.0 / math.sqrt(QK_NOPE + QK_ROPE)
Q_BLOCK = 128
LN_EPS = 1e-5
RMS_EPS = 1e-6

kernel_name = "hybrid_mla_rglru_diffusion_step"


def layer_norm(x, g, b):
    xf = x.astype(jnp.float32)
    mu = jnp.mean(xf, axis=-1, keepdims=True)
    var = jnp.mean(jnp.square(xf - mu), axis=-1, keepdims=True)
    return ((xf - mu) * lax.rsqrt(var + LN_EPS)).astype(x.dtype) * g + b


def rms_norm(x, g):
    xf = x.astype(jnp.float32)
    ms = jnp.mean(jnp.square(xf), axis=-1, keepdims=True)
    return (xf * lax.rsqrt(ms + RMS_EPS)).astype(x.dtype) * g


def modulation(cond, w_mod, b_mod):
    m = jnp.einsum("bd,de->be", jax.nn.silu(cond), w_mod) + b_mod
    return m.reshape(cond.shape[0], N_MOD, D_MODEL)


def swiglu(h, w_up, w_down):
    u = jnp.einsum("btd,df->btf", h, w_up)
    return jnp.einsum("btf,fd->btd", jax.nn.silu(u[..., :D_FF]) * u[..., D_FF:], w_down)


def centred_dwconv(x, w, b):
    t = x.shape[1]
    xp = jnp.pad(x, ((0, 0), (CONV_LEFT, CONV_W - 1 - CONV_LEFT), (0, 0)))
    return sum(xp[:, k:k + t] * w[k] for k in range(CONV_W)) + b


def axial_rope_tables(rows, dtype):
    n_freq = QK_ROPE // 4
    inv = ROPE_BASE ** (-jnp.arange(n_freq, dtype=jnp.float32) / n_freq)
    row = jnp.repeat(jnp.arange(rows, dtype=jnp.float32), GRID_W)
    col = jnp.tile(jnp.arange(GRID_W, dtype=jnp.float32), rows)
    ang = jnp.concatenate([row[:, None] * inv, col[:, None] * inv], axis=-1)
    return jnp.cos(ang).astype(dtype), jnp.sin(ang).astype(dtype)


def apply_rope(x, cos, sin):
    half = QK_ROPE // 2
    x1, x2 = x[..., :half], x[..., half:]
    return jnp.concatenate([x1 * cos - x2 * sin, x1 * sin + x2 * cos], axis=-1)


def mixer_inputs(h, lp):
    b, t, _ = h.shape
    proj = jnp.einsum("btd,de->bte", h, lp["w_in"])
    c_q, c_kv, k_rope, u_x, u_g = jnp.split(proj, IN_SPLITS, axis=-1)
    q = jnp.einsum("btc,ce->bte", rms_norm(c_q, lp["q_norm_g"]), lp["w_uq"])
    q = q.reshape(b, t, N_HEADS, QK_NOPE + QK_ROPE)
    c_kv = rms_norm(c_kv, lp["kv_norm_g"])
    u_x = centred_dwconv(u_x, lp["conv_w"], lp["conv_b"])
    return q[..., :QK_NOPE], q[..., QK_NOPE:], c_kv, k_rope, u_x, u_g


def decompress_kv(c_kv, w_ukv):
    b, t, _ = c_kv.shape
    kv = jnp.einsum("btc,ce->bte", c_kv, w_ukv).reshape(b, t, N_HEADS, QK_NOPE + V_HEAD)
    return kv[..., :QK_NOPE], kv[..., QK_NOPE:]


def mla_attention(q_nope, q_rope, k_nope, k_rope, v):
    b, tq = q_nope.shape[:2]
    blk = math.gcd(tq, Q_BLOCK)
    nb = tq // blk

    def one_block(qs):
        qn, qr = qs
        s = jnp.einsum("bqhd,bkhd->bhqk", qn, k_nope) + jnp.einsum("bqhr,bkr->bhqk", qr, k_rope)
        p = jax.nn.softmax(s.astype(jnp.float32) * ATTN_SCALE, axis=-1).astype(v.dtype)
        return jnp.einsum("bhqk,bkhd->bqhd", p, v)

    qn_b = q_nope.reshape(b, nb, blk, N_HEADS, QK_NOPE).swapaxes(0, 1)
    qr_b = q_rope.reshape(b, nb, blk, N_HEADS, QK_ROPE).swapaxes(0, 1)
    o = lax.map(one_block, (qn_b, qr_b))
    return o.swapaxes(0, 1).reshape(b, tq, MLA_W)


def _linear_combine(e1, e2):
    a1, b1 = e1
    a2, b2 = e2
    return a1 * a2, a2 * b1 + b2


def rg_lru(x, w_a, b_a, w_x, b_x, lam, h0, reverse):
    b, t, _ = x.shape
    xb = x.reshape(b, t, LRU_BLOCKS, LRU_BD)
    r = jax.nn.sigmoid((jnp.einsum("btnd,nde->btne", xb, w_a).reshape(b, t, LRU_W) + b_a).astype(jnp.float32))
    i = jax.nn.sigmoid((jnp.einsum("btnd,nde->btne", xb, w_x).reshape(b, t, LRU_W) + b_x).astype(jnp.float32))
    log_a = -LRU_C * r * jax.nn.softplus(-lam.astype(jnp.float32))
    a = jnp.exp(log_a)
    u = jnp.sqrt(-jnp.expm1(2.0 * log_a)) * (i * x.astype(jnp.float32))
    if reverse:
        a, u = a[:, ::-1], u[:, ::-1]
    u = u.at[:, 0].add(a[:, 0] * h0.astype(jnp.float32))
    _, h = lax.associative_scan(_linear_combine, (a, u), axis=1)
    if reverse:
        h = h[:, ::-1]
    return h.astype(x.dtype)


def bidir_rg_lru(u_x, h0, lp):
    hf = rg_lru(u_x, lp["lru_w_a"][0], lp["lru_b_a"][0], lp["lru_w_x"][0], lp["lru_b_x"][0],
                lp["lru_lambda"][0], h0[:, 0], False)
    hb = rg_lru(u_x, lp["lru_w_a"][1], lp["lru_b_a"][1], lp["lru_w_x"][1], lp["lru_b_x"][1],
                lp["lru_lambda"][1], h0[:, 1], True)
    return hf, hb


def context_mixer(h, lp):
    q_nope, q_rope, c_kv, k_rope, u_x, u_g = mixer_inputs(h, lp)
    k_nope, v = decompress_kv(c_kv, lp["w_ukv"])
    att = mla_attention(q_nope, q_rope, k_nope, k_rope, v)
    h0 = jnp.zeros((h.shape[0], 2, LRU_W), jnp.float32)
    hf, hb = bidir_rg_lru(u_x, h0, lp)
    lru = (hf + hb) * jax.nn.gelu(u_g)
    y = jnp.einsum("bte,ed->btd", jnp.concatenate([att, lru], axis=-1), lp["w_o"])
    final_state = jnp.stack([hf[:, -1], hb[:, 0]], axis=1)
    return y, (c_kv, k_rope, final_state)


def latent_mixer(h, lp, ckv_ctx, krope_ctx, h0, cos, sin):
    q_nope, q_rope, c_kv, k_rope, u_x, u_g = mixer_inputs(h, lp)
    q_rope = apply_rope(q_rope, cos[None, :, None, :], sin[None, :, None, :])
    k_rope = apply_rope(k_rope, cos[None], sin[None])
    k_nope, v = decompress_kv(jnp.concatenate([ckv_ctx, c_kv], axis=1), lp["w_ukv"])
    k_rope = jnp.concatenate([krope_ctx, k_rope], axis=1)
    att = mla_attention(q_nope, q_rope, k_nope, k_rope, v)
    hf, hb = bidir_rg_lru(u_x, h0, lp)
    lru = (hf + hb) * jax.nn.gelu(u_g)
    y = jnp.einsum("bte,ed->btd", jnp.concatenate([att, lru], axis=-1), lp["w_o"])
    return y, None


def trunk_layer(x, mod, mixer, lp):
    m = [mod[:, k, None, :] for k in range(N_MOD)]
    h = x * (1 + m[1]) + m[0]
    x = layer_norm(ALPHA * x + 0.5 * m[2] * swiglu(h, lp["w_ffn_up"][0], lp["w_ffn_down"][0]),
                   lp["ln_g"][0], lp["ln_b"][0])
    h = x * (1 + m[4]) + m[3]
    y, aux = mixer(h)
    x = layer_norm(ALPHA * x + m[5] * y, lp["ln_g"][1], lp["ln_b"][1])
    h = x * (1 + m[7]) + m[6]
    x = layer_norm(ALPHA * x + 0.5 * m[8] * swiglu(h, lp["w_ffn_up"][1], lp["w_ffn_down"][1]),
                   lp["ln_g"][2], lp["ln_b"][2])
    return x, aux


def setup_inputs(seed: int = 0) -> dict:
    key = jax.random.key(seed)
    ks = iter(jax.random.split(key, 32))
    f32 = jnp.float32

    def nrm(shape, s):
        return jax.random.normal(next(ks), shape, f32) * s

    lam_u = jax.random.uniform(next(ks), (DEPTH, 2, LRU_W), f32, 0.9, 0.999)
    return {
        "x_prompt": nrm((BATCH, SEQ, D_MODEL), 1.0),
        "x_sample": nrm((DEC_BATCH, DEC_SEQ, D_MODEL), 1.0),
        "cache_ckv": nrm((DEC_BATCH, DEPTH, PAST_LEN, KV_LORA), 1.0),
        "cache_krope": nrm((DEC_BATCH, DEPTH, PAST_LEN, QK_ROPE), 1.0),
        "state_lru": nrm((DEC_BATCH, DEPTH, 2, LRU_W), 0.5),
        "c": nrm((DEC_BATCH, D_MODEL), 1.0),
        "c_ctx": nrm((D_MODEL,), 1.0),
        "w_mod": nrm((DEPTH, D_MODEL, N_MOD * D_MODEL), 0.5 * D_MODEL ** -0.5),
        "b_mod": nrm((DEPTH, N_MOD * D_MODEL), 0.02),
        "ln_g": 1.0 + nrm((DEPTH, 3, D_MODEL), 0.02),
        "ln_b": nrm((DEPTH, 3, D_MODEL), 0.02),
        "w_ffn_up": nrm((DEPTH, 2, D_MODEL, 2 * D_FF), D_MODEL ** -0.5),
        "w_ffn_down": nrm((DEPTH, 2, D_FF, D_MODEL), BETA * D_FF ** -0.5),
        "w_in": nrm((DEPTH, D_MODEL, IN_W), D_MODEL ** -0.5),
        "q_norm_g": 1.0 + nrm((DEPTH, Q_LORA), 0.02),
        "kv_norm_g": 1.0 + nrm((DEPTH, KV_LORA), 0.02),
        "w_uq": nrm((DEPTH, Q_LORA, N_HEADS * (QK_NOPE + QK_ROPE)), Q_LORA ** -0.5),
        "w_ukv": nrm((DEPTH, KV_LORA, N_HEADS * (QK_NOPE + V_HEAD)), KV_LORA ** -0.5),
        "conv_w": nrm((DEPTH, CONV_W, LRU_W), CONV_W ** -0.5),
        "conv_b": nrm((DEPTH, LRU_W), 0.02),
        "lru_w_a": nrm((DEPTH, 2, LRU_BLOCKS, LRU_BD, LRU_BD), LRU_BD ** -0.5),
        "lru_b_a": nrm((DEPTH, 2, LRU_W), 0.1),
        "lru_w_x": nrm((DEPTH, 2, LRU_BLOCKS, LRU_BD, LRU_BD), LRU_BD ** -0.5),
        "lru_b_x": nrm((DEPTH, 2, LRU_W), 0.1),
        "lru_lambda": jnp.log(lam_u) - jnp.log1p(-lam_u),
        "w_o": nrm((DEPTH, MIX_W, D_MODEL), BETA * MIX_W ** -0.5),
    }


def reference(x_prompt, x_sample, cache_ckv, cache_krope, state_lru, c, c_ctx, w_mod, b_mod,
              ln_g, ln_b, w_ffn_up, w_ffn_down, w_in, q_norm_g, kv_norm_g, w_uq, w_ukv,
              conv_w, conv_b, lru_w_a, lru_b_a, lru_w_x, lru_b_x, lru_lambda, w_o):
    rows = x_sample.shape[1] // GRID_W
    cos, sin = axial_rope_tables(rows, x_sample.dtype)
    xp, xs = x_prompt, x_sample
    ckv_out, krope_out, lru_out = [], [], []
    for l in range(DEPTH):
        lp = {
            "ln_g": ln_g[l], "ln_b": ln_b[l], "w_ffn_up": w_ffn_up[l], "w_ffn_down": w_ffn_down[l],
            "w_in": w_in[l], "q_norm_g": q_norm_g[l], "kv_norm_g": kv_norm_g[l],
            "w_uq": w_uq[l], "w_ukv": w_ukv[l], "conv_w": conv_w[l], "conv_b": conv_b[l],
            "lru_w_a": lru_w_a[l], "lru_b_a": lru_b_a[l], "lru_w_x": lru_w_x[l],
            "lru_b_x": lru_b_x[l], "lru_lambda": lru_lambda[l], "w_o": w_o[l],
        }
        mod_ctx = modulation(c_ctx[None, :], w_mod[l], b_mod[l])
        mod_lat = modulation(c, w_mod[l], b_mod[l])
        xp, (ckv, krope, st) = trunk_layer(xp, mod_ctx, functools.partial(context_mixer, lp=lp), lp)
        ckv_out.append(ckv)
        krope_out.append(krope)
        lru_out.append(st)
        lat_mixer = functools.partial(latent_mixer, lp=lp, ckv_ctx=cache_ckv[:, l],
                                      krope_ctx=cache_krope[:, l], h0=state_lru[:, l], cos=cos, sin=sin)
        xs, _ = trunk_layer(xs, mod_lat, lat_mixer, lp)
    new_cache_ckv = jnp.stack(ckv_out, axis=1)
    new_cache_krope = jnp.stack(krope_out, axis=1)
    new_state_lru = jnp.stack(lru_out, axis=1)
    return (xp, xs, new_cache_ckv, new_cache_krope, new_state_lru)
```

```python
import functools
import math

import jax
import jax.numpy as jnp
from jax import lax
from jax.experimental import pallas as pl
from jax.experimental.pallas import tpu as pltpu

F32 = jnp.float32
BF16 = jnp.bfloat16

D = 1024
BATCH, SEQ = 32, 256
DEC_BATCH, DEC_SEQ = 4, 1024
DEPTH = 4
PAST = 256
GRID_W = 64
H = 8
NOPE, ROPE, VH = 64, 32, 64
KV_LORA, Q_LORA = 256, 384
MLA_W = H * VH
LRU_W, LRU_BLOCKS, LRU_BD = 512, 8, 64
LRU_C = 8.0
D_FF = 2816
N_MOD = 9
ALPHA = (2.0 * DEPTH) ** 0.25
ROPE_BASE = 10000.0
ATTN_SCALE = 1.0 / math.sqrt(NOPE + ROPE)
LN_EPS = 1e-5
RMS_EPS = 1e-6

N_CTX = BATCH * SEQ
N_LAT = DEC_BATCH * DEC_SEQ
N_TOK = N_CTX + N_LAT
SEQ_TILE = 1024
SEG = 8
CTX_PER_TILE = SEQ_TILE // SEQ
N_CTX_TILES = N_CTX // SEQ_TILE
HEAD_LANES = 128
ROPE_HALF = ROPE // 2
FF_CHUNK = 256
N_FF_CHUNKS = D_FF // FF_CHUNK
IN_P = Q_LORA + KV_LORA + 2 * LRU_W + HEAD_LANES
TM = 512
Q_BLK = 256
MOD_ROWS = 8
VMEM_LIMIT = 56 * 1024 * 1024


def _mod_row(i):
    n_ctx = N_CTX // TM
    return jnp.where(i < n_ctx, 0, 1 + (i - n_ctx) // (DEC_SEQ // TM))


def _const_spec(shape):
    nd = len(shape)
    return pl.BlockSpec(shape, lambda *_: (0,) * nd, pipeline_mode=pl.Buffered(1))


def _params(n_grid):
    return pltpu.CompilerParams(dimension_semantics=("arbitrary",) * n_grid,
                                vmem_limit_bytes=VMEM_LIMIT)


def _layer_norm(y, g, b):
    mu = jnp.mean(y, axis=-1, keepdims=True)
    d = y - mu
    var = jnp.mean(d * d, axis=-1, keepdims=True)
    return d * lax.rsqrt(var + LN_EPS) * g + b


def _rms_norm(y, g):
    ms = jnp.mean(y * y, axis=-1, keepdims=True)
    return y * lax.rsqrt(ms + RMS_EPS) * g


def _dot(a, b):
    return jnp.dot(a, b, preferred_element_type=F32)


def _mod_kernel(c_ref, w_ref, b_ref, o_ref):
    s = jax.nn.silu(c_ref[...]).astype(BF16)
    o_ref[0] = _dot(s, w_ref[0].astype(BF16)) + b_ref[0]


def _modulation(conds, w_mod, b_mod):
    return pl.pallas_call(
        _mod_kernel,
        grid=(DEPTH, N_MOD),
        in_specs=[pl.BlockSpec((MOD_ROWS, D), lambda l, j: (0, 0)),
                  pl.BlockSpec((1, D, D), lambda l, j: (l, 0, j)),
                  pl.BlockSpec((1, 1, D), lambda l, j: (l, 0, j))],
        out_specs=pl.BlockSpec((1, MOD_ROWS, D), lambda l, j: (l, 0, j)),
        out_shape=jax.ShapeDtypeStruct((DEPTH, MOD_ROWS, N_MOD * D), F32),
        compiler_params=_params(2),
        name="modulation",
    )(conds, w_mod, b_mod.reshape(DEPTH, 1, N_MOD * D))


def _swiglu_norm(x, m, k0, wup_ref, wdn_ref, g, b):
    h = (x * (1.0 + m[k0 + 1:k0 + 2]) + m[k0:k0 + 1]).astype(BF16)
    acc = jnp.zeros(x.shape, F32)
    for j in range(N_FF_CHUNKS):
        u = _dot(h, wup_ref[j])
        a = (jax.nn.silu(u[:, :FF_CHUNK]) * u[:, FF_CHUNK:]).astype(BF16)
        acc = acc + _dot(a, wdn_ref[j])
    return _layer_norm(ALPHA * x + 0.5 * m[k0 + 2:k0 + 3] * acc, g, b)


def _ffn_kernel(x_ref, mod_ref, wup_ref, wdn_ref, lng_ref, lnb_ref, o_ref):
    o_ref[...] = _swiglu_norm(x_ref[...], mod_ref[0], 0, wup_ref, wdn_ref,
                              lng_ref[0:1], lnb_ref[0:1])


def _mix_ffn_kernel(x_ref, att_ref, lru_ref, wo_ref, mod_ref, wup_ref, wdn_ref, lng_ref, lnb_ref,
                    o_ref):
    m = mod_ref[0]
    y = _dot(att_ref[...], wo_ref[:MLA_W]) + _dot(lru_ref[...], wo_ref[MLA_W:])
    x2 = _layer_norm(ALPHA * x_ref[...] + m[5:6] * y, lng_ref[1:2], lnb_ref[1:2])
    o_ref[...] = _swiglu_norm(x2, m, 6, wup_ref, wdn_ref, lng_ref[2:3], lnb_ref[2:3])


def _row_spec(width, tm=TM, first=0):
    return pl.BlockSpec((tm, width), lambda i: (first + i, 0))


def _mod_spec():
    return pl.BlockSpec((1, N_MOD, D), lambda i: (_mod_row(i), 0, 0))


def _ffn_weight_specs():
    return [_const_spec((N_FF_CHUNKS, D, 2 * FF_CHUNK)), _const_spec((N_FF_CHUNKS, FF_CHUNK, D)),
            _const_spec((3, D)), _const_spec((3, D))]


def _ffn(x, mod, wup, wdn, lng, lnb):
    return pl.pallas_call(
        _ffn_kernel,
        grid=(N_TOK // TM,),
        in_specs=[_row_spec(D), _mod_spec()] + _ffn_weight_specs(),
        out_specs=_row_spec(D),
        out_shape=jax.ShapeDtypeStruct((N_TOK, D), F32),
        compiler_params=_params(1),
        name="ffn",
    )(x, mod, wup, wdn, lng, lnb)


def _mix_ffn(x, att, lru, wo, mod, wup, wdn, lng, lnb):
    return pl.pallas_call(
        _mix_ffn_kernel,
        grid=(N_TOK // TM,),
        in_specs=[_row_spec(D), _row_spec(MLA_W), _row_spec(LRU_W), _const_spec((D, D)), _mod_spec()]
        + _ffn_weight_specs(),
        out_specs=_row_spec(D),
        out_shape=jax.ShapeDtypeStruct((N_TOK, D), F32),
        compiler_params=_params(1),
        name="mix_ffn",
    )(x, att, lru, wo, mod, wup, wdn, lng, lnb)


def _rope(v, tab_ref):
    return (v * tab_ref[0, 0] + pltpu.roll(v, ROPE_HALF, axis=1) * tab_ref[0, 1]
            + pltpu.roll(v, HEAD_LANES - ROPE_HALF, axis=1) * tab_ref[0, 2])


def _proj_kernel(x_ref, mod_ref, tab_ref, win_ref, qg_ref, kvg_ref, wuq_ref,
                 q_ref, ckv_ref, kr_ref, ux_ref, gg_ref):
    m = mod_ref[0]
    h = (x_ref[...] * (1.0 + m[4:5]) + m[3:4]).astype(BF16)
    p = _dot(h, win_ref[...])
    o_kv, o_ux, o_ug, o_kr = Q_LORA, Q_LORA + KV_LORA, Q_LORA + KV_LORA + LRU_W, IN_P - HEAD_LANES
    cq = _rms_norm(p[:, :o_kv], qg_ref[...]).astype(BF16)
    q = _dot(cq, wuq_ref[...])
    for hd in range(H):
        sl = slice(hd * HEAD_LANES, (hd + 1) * HEAD_LANES)
        q_ref[:, sl] = (_rope(q[:, sl], tab_ref) * ATTN_SCALE).astype(BF16)
    ckv_ref[...] = _rms_norm(p[:, o_kv:o_ux], kvg_ref[...])
    kr_ref[...] = _rope(p[:, o_kr:], tab_ref)
    ux_ref[...] = p[:, o_ux:o_ug]
    gg_ref[...] = jax.nn.gelu(p[:, o_ug:o_kr]).astype(BF16)


def _proj(x, mod, tabs, win, qg, kvg, wuq):
    n_ctx = N_CTX // TM
    per_seq = DEC_SEQ // TM

    def tab_map(i):
        lat = i >= n_ctx
        return (jnp.where(lat, 1, 0), 0, jnp.where(lat, (i - n_ctx) % per_seq, 0), 0)

    outs = [(D, BF16), (KV_LORA, F32), (HEAD_LANES, F32), (LRU_W, F32), (LRU_W, BF16)]
    return pl.pallas_call(
        _proj_kernel,
        grid=(N_TOK // TM,),
        in_specs=[_row_spec(D), _mod_spec(),
                  pl.BlockSpec((1, 3, TM, HEAD_LANES), tab_map),
                  _const_spec((D, IN_P)), _const_spec((1, Q_LORA)), _const_spec((1, KV_LORA)),
                  _const_spec((Q_LORA, H * HEAD_LANES))],
        out_specs=[_row_spec(w) for w, _ in outs],
        out_shape=[jax.ShapeDtypeStruct((N_TOK, w), dt) for w, dt in outs],
        compiler_params=_params(1),
        name="proj",
    )(x, mod, tabs, win, qg, kvg, wuq)


def _roll_segments(v, shift):
    groups = [pltpu.roll(v[r:r + SEG], shift, axis=0) for r in range(0, v.shape[0], SEG)]
    return groups[0] if len(groups) == 1 else jnp.concatenate(groups, axis=0)


def _scan_kernel(ux_ref, gg_ref, h0_ref, cw_ref, cb_ref, wg_ref, ba_ref, bx_ref, lam_ref,
                 lru_ref, st_ref, a_scr, u_scr, h_scr, *, seg_len):
    rows = SEQ_TILE // seg_len
    seg = lax.broadcasted_iota(jnp.int32, (rows, LRU_W), 0) % SEG
    starts = seg == 0
    ends = seg == SEG - 1

    x3 = ux_ref[...].reshape(seg_len, rows, LRU_W)
    prev_tail = [jnp.where(starts, 0.0, _roll_segments(x3[seg_len - k], 1)) for k in (2, 1)]
    next_head = jnp.where(ends, 0.0, _roll_segments(x3[0], SEG - 1))
    xm2 = jnp.concatenate([prev_tail[0][None], prev_tail[1][None], x3[:-2]], axis=0)
    xm1 = jnp.concatenate([prev_tail[1][None], x3[:-1]], axis=0)
    xp1 = jnp.concatenate([x3[1:], next_head[None]], axis=0)
    cw = cw_ref[...]
    xc = (xm2 * cw[0:1] + xm1 * cw[1:2] + x3 * cw[2:3] + xp1 * cw[3:4] + cb_ref[...])
    xc = xc.reshape(SEQ_TILE, LRU_W)

    lam = lam_ref[...]
    softplus_neg_lam = jnp.maximum(-lam, 0.0) + jnp.log1p(jnp.exp(-jnp.abs(lam)))
    half = LRU_W // 2
    xcb = xc.astype(BF16)
    for hh in range(2):
        cs = slice(hh * half, (hh + 1) * half)
        z = _dot(xcb[:, cs], wg_ref[hh])
        for d in range(2):
            za = z[:, (2 * d) * half:(2 * d + 1) * half] + ba_ref[d:d + 1, cs]
            zx = z[:, (2 * d + 1) * half:(2 * d + 2) * half] + bx_ref[d:d + 1, cs]
            log_a = -LRU_C * jax.nn.sigmoid(za) * softplus_neg_lam[d:d + 1, cs]
            a = jnp.exp(log_a)
            u = jnp.sqrt(1.0 - a * a) * (jax.nn.sigmoid(zx) * xc[:, cs])
            a_scr[d, :, :, cs] = a.reshape(seg_len, rows, half)
            u_scr[d, :, :, cs] = u.reshape(seg_len, rows, half)

    h0 = h0_ref[0]
    seg8 = lax.broadcasted_iota(jnp.int32, (SEG, LRU_W), 0)
    zeros = jnp.zeros((SEG, LRU_W), F32)
    ones = jnp.ones((SEG, LRU_W), F32)
    for rb in range(rows // SEG):
        rs = slice(rb * SEG, (rb + 1) * SEG)

        def pass1(j, carry):
            hf, pf, hb, pb = carry
            jb = seg_len - 1 - j
            af, ab = a_scr[0, j, rs], a_scr[1, jb, rs]
            return (af * hf + u_scr[0, j, rs], pf * af, ab * hb + u_scr[1, jb, rs], pb * ab)

        hf_end, pf_end, hb_end, pb_end = lax.fori_loop(0, seg_len, pass1, (zeros, ones, zeros, ones),
                                                       unroll=4)

        cin_f = jnp.where(seg8 == 0, h0[0:1], 0.0)
        for s in range(1, SEG):
            nxt = pltpu.roll(pf_end * cin_f + hf_end, 1, axis=0)
            cin_f = jnp.where(seg8 == s, nxt, cin_f)
        cin_b = jnp.where(seg8 == SEG - 1, h0[1:2], 0.0)
        for s in range(SEG - 2, -1, -1):
            nxt = pltpu.roll(pb_end * cin_b + hb_end, SEG - 1, axis=0)
            cin_b = jnp.where(seg8 == s, nxt, cin_b)

        def pass2(j, carry):
            hf, hb = carry
            jb = seg_len - 1 - j
            hf = a_scr[0, j, rs] * hf + u_scr[0, j, rs]
            hb = a_scr[1, jb, rs] * hb + u_scr[1, jb, rs]
            h_scr[0, j, rs] = hf
            h_scr[1, jb, rs] = hb
            return hf, hb

        lax.fori_loop(0, seg_len, pass2, (cin_f, cin_b), unroll=4)

    hsum = (h_scr[0] + h_scr[1]).reshape(SEQ_TILE, LRU_W)
    lru_ref[...] = (hsum * gg_ref[...].astype(F32)).astype(BF16)
    st_ref[0, 0] = h_scr[0, seg_len - 1]
    st_ref[0, 1] = h_scr[1, 0]


def _scan(ux, gg, h0, cw, cb, wg, ba, bx, lam, *, seg_len, first_tile, n_tiles):
    rows = SEQ_TILE // seg_len
    seq = functools.partial(_row_spec, tm=SEQ_TILE, first=first_tile)
    h0_map = (lambda i: (i, 0, 0)) if h0.shape[0] == n_tiles else (lambda i: (0, 0, 0))
    half = LRU_W // 2
    scratch = pltpu.VMEM((2, seg_len, rows, LRU_W), F32)
    return pl.pallas_call(
        functools.partial(_scan_kernel, seg_len=seg_len),
        grid=(n_tiles,),
        in_specs=[seq(LRU_W), seq(LRU_W), pl.BlockSpec((1, 2, LRU_W), h0_map),
                  _const_spec((4, LRU_W)), _const_spec((1, LRU_W)),
                  _const_spec((2, half, 4 * half)),
                  _const_spec((2, LRU_W)), _const_spec((2, LRU_W)), _const_spec((2, LRU_W))],
        out_specs=[_row_spec(LRU_W, tm=SEQ_TILE),
                   pl.BlockSpec((1, 2, rows, LRU_W), lambda i: (i, 0, 0, 0))],
        out_shape=[jax.ShapeDtypeStruct((n_tiles * SEQ_TILE, LRU_W), BF16),
                   jax.ShapeDtypeStruct((n_tiles, 2, rows, LRU_W), F32)],
        scratch_shapes=[scratch, scratch, scratch],
        compiler_params=_params(1),
        name="scan_seg%d" % seg_len,
    )(ux, gg, h0, cw, cb, wg, ba, bx, lam)


def _decompress(ckv, kr, wukv_ref, k_scr, v_scr, r0):
    n = ckv.shape[0]
    kv = _dot(ckv.astype(BF16), wukv_ref[...])
    for hd in range(H):
        k_scr[hd, r0:r0 + n] = (kv[:, hd * HEAD_LANES:(hd + 1) * HEAD_LANES] + kr).astype(BF16)
        v0 = (H + hd) * HEAD_LANES
        v_scr[hd, r0:r0 + n] = kv[:, v0:v0 + HEAD_LANES].astype(BF16)


def _attend(q, k_scr, v_scr):
    outs = []
    for pair in range(H // 2):
        o = None
        for hd in (2 * pair, 2 * pair + 1):
            s = lax.dot_general(q[:, hd * HEAD_LANES:(hd + 1) * HEAD_LANES], k_scr[hd],
                                (((1,), (1,)), ((), ())), preferred_element_type=F32)
            p = jnp.exp(s - jnp.max(s, axis=-1, keepdims=True))
            l = jnp.sum(p, axis=-1, keepdims=True)
            oh = _dot(p.astype(BF16), v_scr[hd]) / l
            o = oh if o is None else o + oh
        outs.append(o)
    return jnp.concatenate(outs, axis=-1)


def _attn_ctx_kernel(q_ref, ckv_ref, kr_ref, wukv_ref, o_ref, k_scr, v_scr):
    _decompress(ckv_ref[...].reshape(SEQ, KV_LORA), kr_ref[...].reshape(SEQ, HEAD_LANES),
                wukv_ref, k_scr, v_scr, 0)
    o = _attend(q_ref[...].reshape(SEQ, D), k_scr, v_scr)
    o_ref[...] = o.astype(BF16).reshape(SEQ // SEG, SEG, MLA_W)


def _attn_lat_kernel(q_ref, ckv_ref, kr_ref, cckv_ref, ckr_ref, wukv_ref, o_ref, k_scr, v_scr):
    @pl.when(pl.program_id(1) == 0)
    def _():
        _decompress(cckv_ref[0], ckr_ref[0], wukv_ref, k_scr, v_scr, 0)
        _decompress(ckv_ref[...], kr_ref[...], wukv_ref, k_scr, v_scr, PAST)

    o_ref[...] = _attend(q_ref[...], k_scr, v_scr).astype(BF16)


def _attention(q, ckv, kr, cache_ckv_l, cache_kr_l, wukv):
    wspec = _const_spec((KV_LORA, 2 * H * HEAD_LANES))
    steps = SEQ // SEG
    grp = CTX_PER_TILE * SEG

    def seq_view(a):
        return a.reshape(N_TOK // grp, CTX_PER_TILE, SEG, a.shape[-1])

    def seq_spec(w):
        return pl.BlockSpec((steps, None, SEG, w), lambda b: (b // CTX_PER_TILE, b % CTX_PER_TILE, 0, 0))

    ctx = pl.pallas_call(
        _attn_ctx_kernel,
        grid=(BATCH,),
        in_specs=[seq_spec(D), seq_spec(KV_LORA), seq_spec(HEAD_LANES), wspec],
        out_specs=seq_spec(MLA_W),
        out_shape=jax.ShapeDtypeStruct((N_CTX // grp, CTX_PER_TILE, SEG, MLA_W), BF16),
        scratch_shapes=[pltpu.VMEM((H, SEQ, HEAD_LANES), BF16), pltpu.VMEM((H, SEQ, HEAD_LANES), BF16)],
        compiler_params=_params(1),
        name="attn_ctx",
    )(seq_view(q), seq_view(ckv), seq_view(kr), wukv).reshape(N_CTX, MLA_W)
    nq = DEC_SEQ // Q_BLK
    q0 = N_CTX // Q_BLK
    s0 = N_CTX // DEC_SEQ
    tk = PAST + DEC_SEQ
    lat = pl.pallas_call(
        _attn_lat_kernel,
        grid=(DEC_BATCH, nq),
        in_specs=[pl.BlockSpec((Q_BLK, D), lambda b, i: (q0 + b * nq + i, 0)),
                  pl.BlockSpec((DEC_SEQ, KV_LORA), lambda b, i: (s0 + b, 0)),
                  pl.BlockSpec((DEC_SEQ, HEAD_LANES), lambda b, i: (s0 + b, 0)),
                  pl.BlockSpec((1, PAST, KV_LORA), lambda b, i: (b, 0, 0)),
                  pl.BlockSpec((1, PAST, HEAD_LANES), lambda b, i: (b, 0, 0)), wspec],
        out_specs=pl.BlockSpec((Q_BLK, MLA_W), lambda b, i: (b * nq + i, 0)),
        out_shape=jax.ShapeDtypeStruct((N_LAT, MLA_W), BF16),
        scratch_shapes=[pltpu.VMEM((H, tk, HEAD_LANES), BF16), pltpu.VMEM((H, tk, HEAD_LANES), BF16)],
        compiler_params=_params(2),
        name="attn_lat",
    )(q, ckv, kr, cache_ckv_l, cache_kr_l, wukv)
    return jnp.concatenate([ctx, lat], axis=0)


def _permute_ctx(x):
    w = x.shape[-1]
    x = x.reshape(N_CTX_TILES, CTX_PER_TILE, SEG, SEQ // SEG, w).transpose(0, 3, 1, 2, 4)
    return x.reshape(N_CTX, w)


def _unpermute_ctx(x):
    w = x.shape[-1]
    x = x.reshape(N_CTX_TILES, SEQ // SEG, CTX_PER_TILE, SEG, w).transpose(0, 2, 3, 1, 4)
    return x.reshape(N_CTX, w)


def _permute_lat(x):
    n, w = x.shape
    return x.reshape(n // DEC_SEQ, SEG, DEC_SEQ // SEG, w).transpose(0, 2, 1, 3).reshape(n, w)


def _unpermute_lat(x):
    n, w = x.shape
    return x.reshape(n // DEC_SEQ, DEC_SEQ // SEG, SEG, w).transpose(0, 2, 1, 3).reshape(n, w)


def _pad_rope_lanes(kr):
    pad = [(0, 0)] * (kr.ndim - 1) + [(NOPE, HEAD_LANES - NOPE - ROPE)]
    return jnp.pad(kr, pad)


def _rope_tables():
    n_freq = ROPE // 4
    inv = ROPE_BASE ** (-jnp.arange(n_freq, dtype=F32) / n_freq)
    t = jnp.arange(DEC_SEQ)
    row = (t // GRID_W).astype(F32)
    col = (t % GRID_W).astype(F32)
    ang = jnp.concatenate([row[:, None] * inv, col[:, None] * inv], axis=-1)
    cos, sin = jnp.cos(ang), jnp.sin(ang)
    ones = jnp.ones((DEC_SEQ, NOPE), F32)
    tail = jnp.ones((DEC_SEQ, HEAD_LANES - NOPE - ROPE), F32)
    z = lambda w: jnp.zeros((DEC_SEQ, w), F32)
    scale = jnp.concatenate([ones, cos, cos, tail], axis=-1)
    from_left = jnp.concatenate([z(NOPE + ROPE_HALF), sin, z(HEAD_LANES - NOPE - ROPE)], axis=-1)
    from_right = jnp.concatenate([z(NOPE), -sin, z(HEAD_LANES - NOPE - ROPE_HALF)], axis=-1)
    lat = jnp.stack([_permute_lat(a) for a in (scale, from_left, from_right)])
    ctx = jnp.stack([jnp.ones_like(scale), jnp.zeros_like(scale), jnp.zeros_like(scale)])
    return jnp.stack([ctx, lat])


def _block_diag_gates(w_a, w_x):
    per_half = LRU_BLOCKS // 2
    eye = jnp.eye(per_half, dtype=w_a.dtype)

    def bd(w):
        w = w.reshape(DEPTH, 2, 2, per_half, LRU_BD, LRU_BD)
        full = jnp.einsum("ldhnij,nm->ldhnimj", w, eye)
        return full.reshape(DEPTH, 2, 2, per_half * LRU_BD, per_half * LRU_BD)

    a, x = bd(w_a), bd(w_x)
    cols = [a[:, 0], x[:, 0], a[:, 1], x[:, 1]]
    return jnp.concatenate(cols, axis=-1).astype(BF16)


def kernel(x_prompt, x_sample, cache_ckv, cache_krope, state_lru, c, c_ctx, w_mod, b_mod, ln_g, ln_b,
           w_ffn_up, w_ffn_down, w_in, q_norm_g, kv_norm_g, w_uq, w_ukv, conv_w, conv_b, lru_w_a,
           lru_b_a, lru_w_x, lru_b_x, lru_lambda, w_o):
    wup = w_ffn_up.reshape(DEPTH, 2, D, 2, N_FF_CHUNKS, FF_CHUNK).transpose(0, 1, 4, 2, 3, 5)
    wup = wup.reshape(DEPTH, 2, N_FF_CHUNKS, D, 2 * FF_CHUNK).astype(BF16)
    wdn = w_ffn_down.reshape(DEPTH, 2, N_FF_CHUNKS, FF_CHUNK, D).astype(BF16)
    o2, o3 = Q_LORA + KV_LORA, Q_LORA + KV_LORA + ROPE
    win = jnp.concatenate([w_in[..., :o2], w_in[..., o3:], _pad_rope_lanes(w_in[..., o2:o3])],
                          axis=-1).astype(BF16)
    wuq = w_uq.reshape(DEPTH, Q_LORA, H, NOPE + ROPE)
    wuq = jnp.pad(wuq, ((0, 0), (0, 0), (0, 0), (0, HEAD_LANES - NOPE - ROPE)))
    wuq = wuq.reshape(DEPTH, Q_LORA, H * HEAD_LANES).astype(BF16)
    wkv = w_ukv.reshape(DEPTH, KV_LORA, H, NOPE + VH)
    wk = jnp.pad(wkv[..., :NOPE], ((0, 0), (0, 0), (0, 0), (0, HEAD_LANES - NOPE)))
    v = wkv[..., NOPE:]
    odd = (jnp.arange(H) % 2 == 1)[None, None, :, None]
    wv = jnp.concatenate([jnp.where(odd, 0.0, v), jnp.where(odd, v, 0.0)], axis=-1)
    wukv = jnp.concatenate([wk.reshape(DEPTH, KV_LORA, H * HEAD_LANES),
                            wv.reshape(DEPTH, KV_LORA, H * HEAD_LANES)], axis=-1).astype(BF16)
    wo = w_o.astype(BF16)
    wg = _block_diag_gates(lru_w_a, lru_w_x)
    cache_kr = _pad_rope_lanes(cache_krope)
    tabs = _rope_tables()
    conds = jnp.concatenate([c_ctx[None], c, jnp.zeros((MOD_ROWS - 1 - DEC_BATCH, D), F32)], axis=0)
    h0_ctx = jnp.zeros((1, 2, LRU_W), F32)

    mod = _modulation(conds, w_mod, b_mod).reshape(DEPTH, MOD_ROWS, N_MOD, D)
    x = jnp.concatenate([_permute_ctx(x_prompt.reshape(N_CTX, D)),
                         _permute_lat(x_sample.reshape(N_LAT, D))], axis=0)

    ckv_out, kr_out, st_out = [], [], []
    for l in range(DEPTH):
        x = _ffn(x, mod[l], wup[l, 0], wdn[l, 0], ln_g[l], ln_b[l])
        q, ckv, kr, ux, gg = _proj(x, mod[l], tabs, win[l], q_norm_g[l][None], kv_norm_g[l][None], wuq[l])
        lru_w = (conv_w[l], conv_b[l][None], wg[l], lru_b_a[l], lru_b_x[l], lru_lambda[l])
        lru_ctx, st = _scan(ux, gg, h0_ctx, *lru_w, seg_len=SEQ // SEG, first_tile=0,
                            n_tiles=N_CTX_TILES)
        lru_lat, _ = _scan(ux, gg, state_lru[:, l], *lru_w, seg_len=DEC_SEQ // SEG,
                           first_tile=N_CTX_TILES, n_tiles=DEC_BATCH)
        lru = jnp.concatenate([lru_ctx, lru_lat], axis=0)
        att = _attention(q, ckv, kr, cache_ckv[:, l], cache_kr[:, l], wukv[l])
        x = _mix_ffn(x, att, lru, wo[l], mod[l], wup[l, 1], wdn[l, 1], ln_g[l], ln_b[l])
        ckv_out.append(ckv[:N_CTX])
        kr_out.append(kr[:N_CTX, NOPE:NOPE + ROPE])
        st_out.append(st)

    y_prompt = _unpermute_ctx(x[:N_CTX]).reshape(BATCH, SEQ, D)
    y_sample = _unpermute_lat(x[N_CTX:]).reshape(DEC_BATCH, DEC_SEQ, D)
    new_ckv = _unpermute_ctx(jnp.concatenate(ckv_out, axis=-1))
    new_ckv = new_ckv.reshape(BATCH, SEQ, DEPTH, KV_LORA).transpose(0, 2, 1, 3)
    new_kr = _unpermute_ctx(jnp.concatenate(kr_out, axis=-1))
    new_kr = new_kr.reshape(BATCH, SEQ, DEPTH, ROPE).transpose(0, 2, 1, 3)
    st = jnp.stack(st_out, axis=0)
    fwd = st[:, :, 0, SEG - 1::SEG].reshape(DEPTH, BATCH, LRU_W)
    bwd = st[:, :, 1, 0::SEG].reshape(DEPTH, BATCH, LRU_W)
    new_state = jnp.stack([fwd, bwd], axis=2).transpose(1, 0, 2, 3)
    return (y_prompt, y_sample, new_ckv, new_kr, new_state)
```

```python
import functools
import math

import jax
import jax.numpy as jnp
from jax import lax
from jax.experimental import pallas as pl
from jax.experimental.pallas import tpu as pltpu

F32 = jnp.float32
BF16 = jnp.bfloat16

D = 1024
BATCH, SEQ = 32, 256
DEC_BATCH, DEC_SEQ = 4, 1024
DEPTH = 4
PAST = 256
GRID_W = 64
H = 8
NOPE, ROPE, VH = 64, 32, 64
KV_LORA, Q_LORA = 256, 384
MLA_W = H * VH
LRU_W, LRU_BLOCKS, LRU_BD = 512, 8, 64
LRU_C = 8.0
D_FF = 2816
N_MOD = 9
ALPHA = (2.0 * DEPTH) ** 0.25
ROPE_BASE = 10000.0
ATTN_SCALE = 1.0 / math.sqrt(NOPE + ROPE)
LN_EPS = 1e-5
RMS_EPS = 1e-6

N_CTX = BATCH * SEQ
N_LAT = DEC_BATCH * DEC_SEQ
N_TOK = N_CTX + N_LAT
SEQ_TILE = 1024
SEG = 8
CTX_PER_TILE = SEQ_TILE // SEQ
N_TILES = N_TOK // SEQ_TILE
N_CTX_TILES = N_CTX // SEQ_TILE
HEAD_LANES = 128
ROPE_HALF = ROPE // 2
FF_CHUNK = 256
N_FF_CHUNKS = D_FF // FF_CHUNK
IN_P = Q_LORA + KV_LORA + 2 * LRU_W + HEAD_LANES
TM = 512
Q_BLK = 256
KEYS_LAT = PAST + DEC_SEQ
MOD_ROWS = 8
VMEM_LIMIT = 56 * 1024 * 1024


def _mod_row(i):
    n_ctx = N_CTX // TM
    return jnp.where(i < n_ctx, 0, 1 + (i - n_ctx) // (DEC_SEQ // TM))


def _const_spec(shape):
    nd = len(shape)
    return pl.BlockSpec(shape, lambda *_: (0,) * nd, pipeline_mode=pl.Buffered(1))


def _layer_spec(shape, l):
    nd = len(shape)
    return pl.BlockSpec((None,) + tuple(shape), lambda *_: (l,) + (0,) * nd, pipeline_mode=pl.Buffered(1))


def _params(n_grid):
    return pltpu.CompilerParams(dimension_semantics=("arbitrary",) * n_grid,
                                vmem_limit_bytes=VMEM_LIMIT)


def _layer_norm(y, g, b):
    mu = jnp.mean(y, axis=-1, keepdims=True)
    d = y - mu
    var = jnp.mean(d * d, axis=-1, keepdims=True)
    return d * lax.rsqrt(var + LN_EPS) * g + b


def _rms_norm(y, g):
    ms = jnp.mean(y * y, axis=-1, keepdims=True)
    return y * lax.rsqrt(ms + RMS_EPS) * g


def _dot(a, b):
    return jnp.dot(a, b, preferred_element_type=F32)


def _mod_kernel(c_ref, w_ref, b_ref, o_ref):
    s = jax.nn.silu(c_ref[...]).astype(BF16)
    o_ref[0] = _dot(s, w_ref[0].astype(BF16)) + b_ref[0]


def _modulation(conds, w_mod, b_mod):
    return pl.pallas_call(
        _mod_kernel,
        grid=(DEPTH, N_MOD),
        in_specs=[pl.BlockSpec((MOD_ROWS, D), lambda l, j: (0, 0)),
                  pl.BlockSpec((1, D, D), lambda l, j: (l, 0, j)),
                  pl.BlockSpec((1, 1, D), lambda l, j: (l, 0, j))],
        out_specs=pl.BlockSpec((1, MOD_ROWS, D), lambda l, j: (l, 0, j)),
        out_shape=jax.ShapeDtypeStruct((DEPTH, MOD_ROWS, N_MOD * D), F32),
        compiler_params=_params(2),
        name="modulation",
    )(conds, w_mod, b_mod.reshape(DEPTH, 1, N_MOD * D))


N_FFN_STEPS = N_FF_CHUNKS - 1 + N_TOK // TM


def _ffn_tile(step):
    return jnp.maximum(step - (N_FF_CHUNKS - 1), 0)


def _ffn_chunk(step):
    return jnp.minimum(step, N_FF_CHUNKS - 1)


def _load_ffn_chunk(step, wa_ref, wb_ref, wd_ref, wup_s, wdn_s):
    @pl.when(step < N_FF_CHUNKS)
    def _():
        wup_s[step, :, :FF_CHUNK] = wa_ref[...].astype(BF16)
        wup_s[step, :, FF_CHUNK:] = wb_ref[...].astype(BF16)
        wdn_s[step] = wd_ref[...].astype(BF16)


def _swiglu_norm(x, m, k0, wup_s, wdn_s, g, b):
    h = (x * (1.0 + m[k0 + 1:k0 + 2]) + m[k0:k0 + 1]).astype(BF16)
    acc = jnp.zeros(x.shape, F32)
    for j in range(N_FF_CHUNKS):
        u = _dot(h, wup_s[j])
        a = (jax.nn.silu(u[:, :FF_CHUNK]) * u[:, FF_CHUNK:]).astype(BF16)
        acc = acc + _dot(a, wdn_s[j])
    return _layer_norm(ALPHA * x + 0.5 * m[k0 + 2:k0 + 3] * acc, g, b)


def _ffn_kernel(x_ref, mod_ref, wa_ref, wb_ref, wd_ref, lng_ref, lnb_ref, o_ref, wup_s, wdn_s):
    step = pl.program_id(0)
    _load_ffn_chunk(step, wa_ref, wb_ref, wd_ref, wup_s, wdn_s)

    @pl.when(step >= N_FF_CHUNKS - 1)
    def _():
        o_ref[...] = _swiglu_norm(x_ref[...], mod_ref[0], 0, wup_s, wdn_s, lng_ref[0:1], lnb_ref[0:1])


def _mix_ffn_kernel(x_ref, att_ref, lru_ref, wo_ref, mod_ref, wa_ref, wb_ref, wd_ref, lng_ref, lnb_ref,
                    o_ref, wup_s, wdn_s, wo_s):
    step = pl.program_id(0)
    _load_ffn_chunk(step, wa_ref, wb_ref, wd_ref, wup_s, wdn_s)

    @pl.when(step == 0)
    def _():
        wo_s[...] = wo_ref[...].astype(BF16)

    @pl.when(step >= N_FF_CHUNKS - 1)
    def _():
        m = mod_ref[0]
        y = _dot(att_ref[...], wo_s[:MLA_W]) + _dot(lru_ref[...], wo_s[MLA_W:])
        x2 = _layer_norm(ALPHA * x_ref[...] + m[5:6] * y, lng_ref[1:2], lnb_ref[1:2])
        o_ref[...] = _swiglu_norm(x2, m, 6, wup_s, wdn_s, lng_ref[2:3], lnb_ref[2:3])


def _row_spec(width, tm=TM):
    return pl.BlockSpec((tm, width), lambda i: (i, 0))


def _ffn_row_spec(width):
    return pl.BlockSpec((TM, width), lambda s: (_ffn_tile(s), 0))


def _mod_spec(l, tile=lambda i: i):
    return pl.BlockSpec((None, 1, N_MOD, D), lambda i: (l, _mod_row(tile(i)), 0, 0))


def _ffn_weight_specs(l, k):
    return [pl.BlockSpec((None, None, D, FF_CHUNK), lambda s: (l, k, 0, _ffn_chunk(s))),
            pl.BlockSpec((None, None, D, FF_CHUNK), lambda s: (l, k, 0, N_FF_CHUNKS + _ffn_chunk(s))),
            pl.BlockSpec((None, None, FF_CHUNK, D), lambda s: (l, k, _ffn_chunk(s), 0)),
            _layer_spec((3, D), l), _layer_spec((3, D), l)]


_FFN_SCRATCH = [pltpu.VMEM((N_FF_CHUNKS, D, 2 * FF_CHUNK), BF16), pltpu.VMEM((N_FF_CHUNKS, FF_CHUNK, D), BF16)]


def _ffn(l, x, mod, w_up, w_down, ln_g, ln_b):
    return pl.pallas_call(
        _ffn_kernel,
        grid=(N_FFN_STEPS,),
        in_specs=[_ffn_row_spec(D), _mod_spec(l, _ffn_tile)] + _ffn_weight_specs(l, 0),
        out_specs=_ffn_row_spec(D),
        out_shape=jax.ShapeDtypeStruct((N_TOK, D), F32),
        scratch_shapes=_FFN_SCRATCH,
        compiler_params=_params(1),
        name="ffn",
    )(x, mod, w_up, w_up, w_down, ln_g, ln_b)


def _mix_ffn(l, x, att, lru, w_o, mod, w_up, w_down, ln_g, ln_b):
    return pl.pallas_call(
        _mix_ffn_kernel,
        grid=(N_FFN_STEPS,),
        in_specs=[_ffn_row_spec(D), _ffn_row_spec(MLA_W), _ffn_row_spec(LRU_W), _layer_spec((D, D), l),
                  _mod_spec(l, _ffn_tile)] + _ffn_weight_specs(l, 1),
        out_specs=_ffn_row_spec(D),
        out_shape=jax.ShapeDtypeStruct((N_TOK, D), F32),
        scratch_shapes=_FFN_SCRATCH + [pltpu.VMEM((D, D), BF16)],
        compiler_params=_params(1),
        name="mix_ffn",
    )(x, att, lru, w_o, mod, w_up, w_up, w_down, ln_g, ln_b)


def _rope(v, tab_ref):
    return (v * tab_ref[0, 0] + pltpu.roll(v, ROPE_HALF, axis=1) * tab_ref[0, 1]
            + pltpu.roll(v, HEAD_LANES - ROPE_HALF, axis=1) * tab_ref[0, 2])


def _proj_kernel(x_ref, mod_ref, tab_ref, win_ref, qg_ref, kvg_ref, wuq_ref,
                 q_ref, ckv_ref, kr_ref, ux_ref, gg_ref):
    m = mod_ref[0]
    h = (x_ref[...] * (1.0 + m[4:5]) + m[3:4]).astype(BF16)
    p = _dot(h, win_ref[...])
    o_kv, o_ux, o_ug, o_kr = Q_LORA, Q_LORA + KV_LORA, Q_LORA + KV_LORA + LRU_W, IN_P - HEAD_LANES
    cq = _rms_norm(p[:, :o_kv], qg_ref[...]).astype(BF16)
    q = _dot(cq, wuq_ref[...])
    for hd in range(H):
        sl = slice(hd * HEAD_LANES, (hd + 1) * HEAD_LANES)
        q_ref[:, sl] = (_rope(q[:, sl], tab_ref) * ATTN_SCALE).astype(BF16)
    ckv_ref[...] = _rms_norm(p[:, o_kv:o_ux], kvg_ref[...])
    kr_ref[...] = _rope(p[:, o_kr:], tab_ref)
    ux_ref[...] = p[:, o_ux:o_ug]
    gg_ref[...] = jax.nn.gelu(p[:, o_ug:o_kr]).astype(BF16)


def _proj(l, x, mod, tabs, win, qg, kvg, wuq):
    n_ctx = N_CTX // TM
    per_seq = DEC_SEQ // TM

    def tab_map(i):
        lat = i >= n_ctx
        return (jnp.where(lat, 1, 0), 0, jnp.where(lat, (i - n_ctx) % per_seq, 0), 0)

    outs = [(D, BF16), (KV_LORA, F32), (HEAD_LANES, F32), (LRU_W, F32), (LRU_W, BF16)]
    return pl.pallas_call(
        _proj_kernel,
        grid=(N_TOK // TM,),
        in_specs=[_row_spec(D), _mod_spec(l),
                  pl.BlockSpec((1, 3, TM, HEAD_LANES), tab_map),
                  _layer_spec((D, IN_P), l), _layer_spec((1, Q_LORA), l), _layer_spec((1, KV_LORA), l),
                  _layer_spec((Q_LORA, H * HEAD_LANES), l)],
        out_specs=[_row_spec(w) for w, _ in outs],
        out_shape=[jax.ShapeDtypeStruct((N_TOK, w), dt) for w, dt in outs],
        compiler_params=_params(1),
        name="proj",
    )(x, mod, tabs, win, qg, kvg, wuq)


def _conv_tile(ux_ref, cw_ref, cb_ref, xc_scr, seq_len):
    seg_len = seq_len // SEG
    seg = lax.broadcasted_iota(jnp.int32, (SEG, LRU_W), 0)
    cw = cw_ref[...]
    for r0 in range(0, SEQ_TILE, seq_len):
        x3 = ux_ref[r0:r0 + seq_len].reshape(seg_len, SEG, LRU_W)
        prev_tail = [jnp.where(seg == 0, 0.0, pltpu.roll(x3[seg_len - k], 1, axis=0)) for k in (2, 1)]
        next_head = jnp.where(seg == SEG - 1, 0.0, pltpu.roll(x3[0], SEG - 1, axis=0))
        xm2 = jnp.concatenate([prev_tail[0][None], prev_tail[1][None], x3[:-2]], axis=0)
        xm1 = jnp.concatenate([prev_tail[1][None], x3[:-1]], axis=0)
        xp1 = jnp.concatenate([x3[1:], next_head[None]], axis=0)
        xc = xm2 * cw[0:1] + xm1 * cw[1:2] + x3 * cw[2:3] + xp1 * cw[3:4] + cb_ref[...]
        xc_scr[r0:r0 + seq_len] = xc.reshape(seq_len, LRU_W)


def _scan_tile(h0, a_scr, u_scr, h_scr, seq_len):
    seg_len = seq_len // SEG
    seg = lax.broadcasted_iota(jnp.int32, (SEG, LRU_W), 0)
    zeros = jnp.zeros((SEG, LRU_W), F32)
    ones = jnp.ones((SEG, LRU_W), F32)
    for r0 in range(0, SEQ_TILE, seq_len):
        def rows(j):
            return pl.ds(pl.multiple_of(r0 + j * SEG, SEG), SEG)

        def pass1(j, carry):
            hf, pf, hb, pb = carry
            rf, rb = rows(j), rows(seg_len - 1 - j)
            af, ab = a_scr[0, rf], a_scr[1, rb]
            return (af * hf + u_scr[0, rf], pf * af, ab * hb + u_scr[1, rb], pb * ab)

        hf_end, pf_end, hb_end, pb_end = lax.fori_loop(0, seg_len, pass1, (zeros, ones, zeros, ones),
                                                       unroll=4)

        cin_f = jnp.where(seg == 0, h0[0:1], 0.0)
        for s in range(1, SEG):
            nxt = pltpu.roll(pf_end * cin_f + hf_end, 1, axis=0)
            cin_f = jnp.where(seg == s, nxt, cin_f)
        cin_b = jnp.where(seg == SEG - 1, h0[1:2], 0.0)
        for s in range(SEG - 2, -1, -1):
            nxt = pltpu.roll(pb_end * cin_b + hb_end, SEG - 1, axis=0)
            cin_b = jnp.where(seg == s, nxt, cin_b)

        def pass2(j, carry):
            hf, hb = carry
            rf, rb = rows(j), rows(seg_len - 1 - j)
            hf = a_scr[0, rf] * hf + u_scr[0, rf]
            hb = a_scr[1, rb] * hb + u_scr[1, rb]
            h_scr[0, rf] = hf
            h_scr[1, rb] = hb
            return hf, hb

        lax.fori_loop(0, seg_len, pass2, (cin_f, cin_b), unroll=4)


def _scan_kernel(ux_ref, gg_ref, h0_ref, cw_ref, cb_ref, wg_ref, ba_ref, bx_ref, lam_ref,
                 lru_ref, st_ref, xc_scr, a_scr, u_scr, h_scr):
    is_ctx = pl.program_id(0) < N_CTX_TILES

    @pl.when(is_ctx)
    def _():
        _conv_tile(ux_ref, cw_ref, cb_ref, xc_scr, SEQ)

    @pl.when(jnp.logical_not(is_ctx))
    def _():
        _conv_tile(ux_ref, cw_ref, cb_ref, xc_scr, DEC_SEQ)

    lam = lam_ref[...]
    softplus_neg_lam = jnp.maximum(-lam, 0.0) + jnp.log1p(jnp.exp(-jnp.abs(lam)))
    half = LRU_W // 2
    for hh in range(2):
        cs = slice(hh * half, (hh + 1) * half)
        xc = xc_scr[:, cs]
        z = _dot(xc.astype(BF16), wg_ref[hh])
        for d in range(2):
            za = z[:, (2 * d) * half:(2 * d + 1) * half] + ba_ref[d:d + 1, cs]
            zx = z[:, (2 * d + 1) * half:(2 * d + 2) * half] + bx_ref[d:d + 1, cs]
            log_a = -LRU_C * jax.nn.sigmoid(za) * softplus_neg_lam[d:d + 1, cs]
            a = jnp.exp(log_a)
            a_scr[d, :, cs] = a
            u_scr[d, :, cs] = jnp.sqrt(1.0 - a * a) * (jax.nn.sigmoid(zx) * xc)

    @pl.when(is_ctx)
    def _():
        _scan_tile(jnp.zeros((2, LRU_W), F32), a_scr, u_scr, h_scr, SEQ)
        for q in range(CTX_PER_TILE):
            st_ref[0, 0, q * SEG:(q + 1) * SEG] = h_scr[0, (q + 1) * SEQ - SEG:(q + 1) * SEQ]
            st_ref[0, 1, q * SEG:(q + 1) * SEG] = h_scr[1, q * SEQ:q * SEQ + SEG]

    @pl.when(jnp.logical_not(is_ctx))
    def _():
        _scan_tile(h0_ref[0], a_scr, u_scr, h_scr, DEC_SEQ)
        st_ref[...] = jnp.zeros(st_ref.shape, F32)

    lru_ref[...] = ((h_scr[0] + h_scr[1]) * gg_ref[...].astype(F32)).astype(BF16)


def _scan(l, ux, gg, h0, cw, cb, wg, ba, bx, lam):
    seq = functools.partial(_row_spec, tm=SEQ_TILE)
    half = LRU_W // 2
    big = pltpu.VMEM((2, SEQ_TILE, LRU_W), F32)
    return pl.pallas_call(
        _scan_kernel,
        grid=(N_TILES,),
        in_specs=[seq(LRU_W), seq(LRU_W),
                  pl.BlockSpec((None, 1, 2, LRU_W), lambda i: (l, jnp.maximum(i - N_CTX_TILES + 1, 0), 0, 0)),
                  _layer_spec((4, LRU_W), l), _layer_spec((1, LRU_W), l), _layer_spec((2, half, 4 * half), l),
                  _layer_spec((2, LRU_W), l), _layer_spec((2, LRU_W), l), _layer_spec((2, LRU_W), l)],
        out_specs=[seq(LRU_W), pl.BlockSpec((1, 2, CTX_PER_TILE * SEG, LRU_W), lambda i: (i, 0, 0, 0))],
        out_shape=[jax.ShapeDtypeStruct((N_TOK, LRU_W), BF16),
                   jax.ShapeDtypeStruct((N_TILES, 2, CTX_PER_TILE * SEG, LRU_W), F32)],
        scratch_shapes=[pltpu.VMEM((SEQ_TILE, LRU_W), F32), big, big, big],
        compiler_params=_params(1),
        name="scan",
    )(ux, gg, h0, cw, cb, wg, ba, bx, lam)


def _decompress(ckv, kr, wukv_ref, k_scr, v_scr, r0):
    n = ckv.shape[0]
    kv = _dot(ckv.astype(BF16), wukv_ref[...])
    for hd in range(H):
        k_scr[hd, r0:r0 + n] = (kv[:, hd * HEAD_LANES:(hd + 1) * HEAD_LANES] + kr).astype(BF16)
        v0 = (H + hd) * HEAD_LANES
        v_scr[hd, r0:r0 + n] = kv[:, v0:v0 + HEAD_LANES].astype(BF16)


def _attend(q, k_scr, v_scr, n_keys):
    outs = []
    for pair in range(H // 2):
        o = None
        for hd in (2 * pair, 2 * pair + 1):
            s = lax.dot_general(q[:, hd * HEAD_LANES:(hd + 1) * HEAD_LANES], k_scr[hd, :n_keys],
                                (((1,), (1,)), ((), ())), preferred_element_type=F32)
            p = jnp.exp(s - jnp.max(s, axis=-1, keepdims=True))
            l = jnp.sum(p, axis=-1, keepdims=True)
            oh = _dot(p.astype(BF16), v_scr[hd, :n_keys]) / l
            o = oh if o is None else o + oh
        outs.append(o)
    return jnp.concatenate(outs, axis=-1)


def _attn_kernel(q_ref, ckv_ref, kr_ref, cckv_ref, ckr_ref, wukv_ref, o_ref, k_scr, v_scr):
    is_ctx = pl.program_id(0) < N_CTX_TILES

    @pl.when(is_ctx)
    def _():
        def one_seq(i, carry):
            rows = pl.ds(pl.multiple_of(i * SEQ, SEQ), SEQ)
            _decompress(ckv_ref[rows], kr_ref[rows], wukv_ref, k_scr, v_scr, 0)
            o_ref[rows] = _attend(q_ref[rows], k_scr, v_scr, SEQ).astype(BF16)
            return carry

        lax.fori_loop(0, CTX_PER_TILE, one_seq, 0)

    @pl.when(jnp.logical_not(is_ctx))
    def _():
        _decompress(cckv_ref[0], ckr_ref[0], wukv_ref, k_scr, v_scr, 0)
        _decompress(ckv_ref[...], kr_ref[...], wukv_ref, k_scr, v_scr, PAST)

        def one_block(i, carry):
            rows = pl.ds(pl.multiple_of(i * Q_BLK, Q_BLK), Q_BLK)
            o_ref[rows] = _attend(q_ref[rows], k_scr, v_scr, KEYS_LAT).astype(BF16)
            return carry

        lax.fori_loop(0, DEC_SEQ // Q_BLK, one_block, 0)


def _attention(l, q, ckv, kr, cache_ckv, cache_kr, wukv):
    seq = functools.partial(_row_spec, tm=SEQ_TILE)

    def cache_spec(w):
        return pl.BlockSpec((1, None, PAST, w), lambda i: (jnp.maximum(i - N_CTX_TILES, 0), l, 0, 0))

    kv_scratch = pltpu.VMEM((H, KEYS_LAT, HEAD_LANES), BF16)
    return pl.pallas_call(
        _attn_kernel,
        grid=(N_TILES,),
        in_specs=[seq(D), seq(KV_LORA), seq(HEAD_LANES), cache_spec(KV_LORA), cache_spec(HEAD_LANES),
                  _layer_spec((KV_LORA, 2 * H * HEAD_LANES), l)],
        out_specs=seq(MLA_W),
        out_shape=jax.ShapeDtypeStruct((N_TOK, MLA_W), BF16),
        scratch_shapes=[kv_scratch, kv_scratch],
        compiler_params=_params(1),
        name="attn",
    )(q, ckv, kr, cache_ckv, cache_kr, wukv)


def _permute_seqs(x, seq_len):
    n, w = x.shape
    return x.reshape(n // seq_len, SEG, seq_len // SEG, w).transpose(0, 2, 1, 3).reshape(n, w)


def _unpermute_seqs(x, seq_len):
    n, w = x.shape
    return x.reshape(n // seq_len, seq_len // SEG, SEG, w).transpose(0, 2, 1, 3).reshape(n, w)


def _pad_rope_lanes(kr):
    pad = [(0, 0)] * (kr.ndim - 1) + [(NOPE, HEAD_LANES - NOPE - ROPE)]
    return jnp.pad(kr, pad)


def _rope_tables():
    n_freq = ROPE // 4
    inv = ROPE_BASE ** (-jnp.arange(n_freq, dtype=F32) / n_freq)
    t = jnp.arange(DEC_SEQ)
    row = (t // GRID_W).astype(F32)
    col = (t % GRID_W).astype(F32)
    ang = jnp.concatenate([row[:, None] * inv, col[:, None] * inv], axis=-1)
    cos, sin = jnp.cos(ang), jnp.sin(ang)
    ones = jnp.ones((DEC_SEQ, NOPE), F32)
    tail = jnp.ones((DEC_SEQ, HEAD_LANES - NOPE - ROPE), F32)
    z = lambda w: jnp.zeros((DEC_SEQ, w), F32)
    scale = jnp.concatenate([ones, cos, cos, tail], axis=-1)
    from_left = jnp.concatenate([z(NOPE + ROPE_HALF), sin, z(HEAD_LANES - NOPE - ROPE)], axis=-1)
    from_right = jnp.concatenate([z(NOPE), -sin, z(HEAD_LANES - NOPE - ROPE_HALF)], axis=-1)
    lat = jnp.stack([_permute_seqs(a, DEC_SEQ) for a in (scale, from_left, from_right)])
    ctx = jnp.stack([jnp.ones_like(scale), jnp.zeros_like(scale), jnp.zeros_like(scale)])
    return jnp.stack([ctx, lat])


def _block_diag_gates(w_a, w_x):
    per_half = LRU_BLOCKS // 2
    eye = jnp.eye(per_half, dtype=w_a.dtype)

    def bd(w):
        w = w.reshape(DEPTH, 2, 2, per_half, LRU_BD, LRU_BD)
        full = jnp.einsum("ldhnij,nm->ldhnimj", w, eye)
        return full.reshape(DEPTH, 2, 2, per_half * LRU_BD, per_half * LRU_BD)

    a, x = bd(w_a), bd(w_x)
    cols = [a[:, 0], x[:, 0], a[:, 1], x[:, 1]]
    return jnp.concatenate(cols, axis=-1).astype(BF16)


def kernel(x_prompt, x_sample, cache_ckv, cache_krope, state_lru, c, c_ctx, w_mod, b_mod, ln_g, ln_b,
           w_ffn_up, w_ffn_down, w_in, q_norm_g, kv_norm_g, w_uq, w_ukv, conv_w, conv_b, lru_w_a,
           lru_b_a, lru_w_x, lru_b_x, lru_lambda, w_o):
    o2, o3 = Q_LORA + KV_LORA, Q_LORA + KV_LORA + ROPE
    win = jnp.concatenate([w_in[..., :o2], w_in[..., o3:], _pad_rope_lanes(w_in[..., o2:o3])],
                          axis=-1).astype(BF16)
    wuq = w_uq.reshape(DEPTH, Q_LORA, H, NOPE + ROPE)
    wuq = jnp.pad(wuq, ((0, 0), (0, 0), (0, 0), (0, HEAD_LANES - NOPE - ROPE)))
    wuq = wuq.reshape(DEPTH, Q_LORA, H * HEAD_LANES).astype(BF16)
    wkv = w_ukv.reshape(DEPTH, KV_LORA, H, NOPE + VH)
    wk = jnp.pad(wkv[..., :NOPE], ((0, 0), (0, 0), (0, 0), (0, HEAD_LANES - NOPE)))
    v = wkv[..., NOPE:]
    odd = (jnp.arange(H) % 2 == 1)[None, None, :, None]
    wv = jnp.concatenate([jnp.where(odd, 0.0, v), jnp.where(odd, v, 0.0)], axis=-1)
    wukv = jnp.concatenate([wk.reshape(DEPTH, KV_LORA, H * HEAD_LANES),
                            wv.reshape(DEPTH, KV_LORA, H * HEAD_LANES)], axis=-1).astype(BF16)
    wg = _block_diag_gates(lru_w_a, lru_w_x)
    cache_kr = _pad_rope_lanes(cache_krope)
    tabs = _rope_tables()
    conds = jnp.concatenate([c_ctx[None], c, jnp.zeros((MOD_ROWS - 1 - DEC_BATCH, D), F32)], axis=0)
    h0 = jnp.concatenate([jnp.zeros((DEPTH, 1, 2, LRU_W), F32), state_lru.transpose(1, 0, 2, 3)], axis=1)
    qg, kvg, cb = q_norm_g[:, None], kv_norm_g[:, None], conv_b[:, None]

    mod = _modulation(conds, w_mod, b_mod).reshape(DEPTH, MOD_ROWS, N_MOD, D)
    x = jnp.concatenate([_permute_seqs(x_prompt.reshape(N_CTX, D), SEQ),
                         _permute_seqs(x_sample.reshape(N_LAT, D), DEC_SEQ)], axis=0)

    ckv_out, kr_out, st_out = [], [], []
    for l in range(DEPTH):
        x = _ffn(l, x, mod, w_ffn_up, w_ffn_down, ln_g, ln_b)
        q, ckv, kr, ux, gg = _proj(l, x, mod, tabs, win, qg, kvg, wuq)
        lru, st = _scan(l, ux, gg, h0, conv_w, cb, wg, lru_b_a, lru_b_x, lru_lambda)
        att = _attention(l, q, ckv, kr, cache_ckv, cache_kr, wukv)
        x = _mix_ffn(l, x, att, lru, w_o, mod, w_ffn_up, w_ffn_down, ln_g, ln_b)
        ckv_out.append(ckv[:N_CTX])
        kr_out.append(kr[:N_CTX, NOPE:NOPE + ROPE])
        st_out.append(st[:N_CTX_TILES])

    y_prompt = _unpermute_seqs(x[:N_CTX], SEQ).reshape(BATCH, SEQ, D)
    y_sample = _unpermute_seqs(x[N_CTX:], DEC_SEQ).reshape(DEC_BATCH, DEC_SEQ, D)
    new_ckv = _unpermute_seqs(jnp.concatenate(ckv_out, axis=-1), SEQ)
    new_ckv = new_ckv.reshape(BATCH, SEQ, DEPTH, KV_LORA).transpose(0, 2, 1, 3)
    new_kr = _unpermute_seqs(jnp.concatenate(kr_out, axis=-1), SEQ)
    new_kr = new_kr.reshape(BATCH, SEQ, DEPTH, ROPE).transpose(0, 2, 1, 3)
    st = jnp.stack(st_out, axis=0)
    fwd = st[:, :, 0, SEG - 1::SEG].reshape(DEPTH, BATCH, LRU_W)
    bwd = st[:, :, 1, 0::SEG].reshape(DEPTH, BATCH, LRU_W)
    new_state = jnp.stack([fwd, bwd], axis=2).transpose(1, 0, 2, 3)
    return (y_prompt, y_sample, new_ckv, new_kr, new_state)
```

```python
import functools
import math

import jax
import jax.numpy as jnp
from jax import lax
from jax.experimental import pallas as pl
from jax.experimental.pallas import tpu as pltpu

F32 = jnp.float32
BF16 = jnp.bfloat16

D = 1024
BATCH, SEQ = 32, 256
DEC_BATCH, DEC_SEQ = 4, 1024
DEPTH = 4
PAST = 256
GRID_W = 64
H = 8
NOPE, ROPE, VH = 64, 32, 64
KV_LORA, Q_LORA = 256, 384
MLA_W = H * VH
LRU_W, LRU_BLOCKS, LRU_BD = 512, 8, 64
LRU_C = 8.0
D_FF = 2816
N_MOD = 9
ALPHA = (2.0 * DEPTH) ** 0.25
ROPE_BASE = 10000.0
ATTN_SCALE = 1.0 / math.sqrt(NOPE + ROPE)
LN_EPS = 1e-5
RMS_EPS = 1e-6

N_CTX = BATCH * SEQ
N_LAT = DEC_BATCH * DEC_SEQ
N_TOK = N_CTX + N_LAT
SEQ_TILE = 1024
SEG = 8
CTX_PER_TILE = SEQ_TILE // SEQ
N_TILES = N_TOK // SEQ_TILE
N_CTX_TILES = N_CTX // SEQ_TILE
HEAD_LANES = 128
ROPE_HALF = ROPE // 2
FF_CHUNK = 256
N_FF_CHUNKS = D_FF // FF_CHUNK
IN_P = Q_LORA + KV_LORA + 2 * LRU_W + HEAD_LANES
TM = 512
Q_BLK = 256
KEYS_LAT = PAST + DEC_SEQ
MOD_ROWS = 8
VMEM_LIMIT = 56 * 1024 * 1024


def _mod_row(i):
    n_ctx = N_CTX // TM
    return jnp.where(i < n_ctx, 0, 1 + (i - n_ctx) // (DEC_SEQ // TM))


def _const_spec(shape):
    nd = len(shape)
    return pl.BlockSpec(shape, lambda *_: (0,) * nd, pipeline_mode=pl.Buffered(1))


def _layer_spec(shape, l):
    nd = len(shape)
    return pl.BlockSpec((None,) + tuple(shape), lambda *_: (l,) + (0,) * nd, pipeline_mode=pl.Buffered(1))


def _params(n_grid):
    return pltpu.CompilerParams(dimension_semantics=("arbitrary",) * n_grid,
                                vmem_limit_bytes=VMEM_LIMIT)


def _layer_norm(y, g, b):
    mu = jnp.mean(y, axis=-1, keepdims=True)
    d = y - mu
    var = jnp.mean(d * d, axis=-1, keepdims=True)
    return d * lax.rsqrt(var + LN_EPS) * g + b


def _rms_norm(y, g):
    ms = jnp.mean(y * y, axis=-1, keepdims=True)
    return y * lax.rsqrt(ms + RMS_EPS) * g


def _dot(a, b):
    return jnp.dot(a, b, preferred_element_type=F32)


def _mod_kernel(c_ref, w_ref, b_ref, o_ref):
    s = jax.nn.silu(c_ref[...]).astype(BF16)
    o_ref[0] = _dot(s, w_ref[0].astype(BF16)) + b_ref[0]


def _modulation(conds, w_mod, b_mod):
    return pl.pallas_call(
        _mod_kernel,
        grid=(DEPTH, N_MOD),
        in_specs=[pl.BlockSpec((MOD_ROWS, D), lambda l, j: (0, 0)),
                  pl.BlockSpec((1, D, D), lambda l, j: (l, 0, j)),
                  pl.BlockSpec((1, 1, D), lambda l, j: (l, 0, j))],
        out_specs=pl.BlockSpec((1, MOD_ROWS, D), lambda l, j: (l, 0, j)),
        out_shape=jax.ShapeDtypeStruct((DEPTH, MOD_ROWS, N_MOD * D), F32),
        compiler_params=_params(2),
        name="modulation",
    )(conds, w_mod, b_mod.reshape(DEPTH, 1, N_MOD * D))


N_FFN_STEPS = N_FF_CHUNKS - 1 + N_TOK // TM


def _ffn_tile(step):
    return jnp.maximum(step - (N_FF_CHUNKS - 1), 0)


def _ffn_chunk(step):
    return jnp.minimum(step, N_FF_CHUNKS - 1)


def _load_ffn_chunk(step, wa_ref, wb_ref, wd_ref, wup_s, wdn_s):
    @pl.when(step < N_FF_CHUNKS)
    def _():
        wup_s[step, :, :FF_CHUNK] = wa_ref[...].astype(BF16)
        wup_s[step, :, FF_CHUNK:] = wb_ref[...].astype(BF16)
        wdn_s[step] = wd_ref[...].astype(BF16)


def _swiglu_norm(x, m, k0, wup_s, wdn_s, g, b):
    h = (x * (1.0 + m[k0 + 1:k0 + 2]) + m[k0:k0 + 1]).astype(BF16)
    acc = jnp.zeros(x.shape, F32)
    for j in range(N_FF_CHUNKS):
        u = _dot(h, wup_s[j])
        a = (jax.nn.silu(u[:, :FF_CHUNK]) * u[:, FF_CHUNK:]).astype(BF16)
        acc = acc + _dot(a, wdn_s[j])
    return _layer_norm(ALPHA * x + 0.5 * m[k0 + 2:k0 + 3] * acc, g, b)


N_CTX_ROW_TILES = N_CTX // TM


def _ffn_kernel(xc_ref, xl_ref, mod_ref, wa_ref, wb_ref, wd_ref, lng_ref, lnb_ref, o_ref, wup_s, wdn_s):
    step = pl.program_id(0)
    _load_ffn_chunk(step, wa_ref, wb_ref, wd_ref, wup_s, wdn_s)

    @pl.when(step >= N_FF_CHUNKS - 1)
    def _():
        x = jnp.where(_ffn_tile(step) < N_CTX_ROW_TILES, xc_ref[...], xl_ref[...])
        o_ref[...] = _swiglu_norm(x, mod_ref[0], 0, wup_s, wdn_s, lng_ref[0:1], lnb_ref[0:1])


def _mix_ffn_kernel(x_ref, att_ref, lru_ref, wo_ref, mod_ref, wa_ref, wb_ref, wd_ref, lng_ref, lnb_ref,
                    oc_ref, ol_ref, wup_s, wdn_s, wo_s, y_s):
    step = pl.program_id(0)
    _load_ffn_chunk(step, wa_ref, wb_ref, wd_ref, wup_s, wdn_s)

    @pl.when(step == 0)
    def _():
        wo_s[...] = wo_ref[...].astype(BF16)

    computing = step >= N_FF_CHUNKS - 1
    is_ctx = _ffn_tile(step) < N_CTX_ROW_TILES

    @pl.when(computing)
    def _():
        m = mod_ref[0]
        y = _dot(att_ref[...], wo_s[:MLA_W]) + _dot(lru_ref[...], wo_s[MLA_W:])
        x2 = _layer_norm(ALPHA * x_ref[...] + m[5:6] * y, lng_ref[1:2], lnb_ref[1:2])
        y_s[...] = _swiglu_norm(x2, m, 6, wup_s, wdn_s, lng_ref[2:3], lnb_ref[2:3])

    @pl.when(jnp.logical_and(computing, is_ctx))
    def _():
        oc_ref[...] = y_s[...]

    @pl.when(jnp.logical_not(is_ctx))
    def _():
        ol_ref[...] = y_s[...]


def _row_spec(width, tm=TM):
    return pl.BlockSpec((tm, width), lambda i: (i, 0))


def _ffn_row_spec(width):
    return pl.BlockSpec((TM, width), lambda s: (_ffn_tile(s), 0))


def _ffn_ctx_spec():
    return pl.BlockSpec((TM, D), lambda s: (jnp.minimum(_ffn_tile(s), N_CTX_ROW_TILES - 1), 0))


def _ffn_lat_spec():
    return pl.BlockSpec((TM, D), lambda s: (jnp.maximum(_ffn_tile(s) - N_CTX_ROW_TILES, 0), 0))


def _mod_spec(l, tile=lambda i: i):
    return pl.BlockSpec((None, 1, N_MOD, D), lambda i: (l, _mod_row(tile(i)), 0, 0))


def _ffn_weight_specs(l, k):
    return [pl.BlockSpec((None, None, D, FF_CHUNK), lambda s: (l, k, 0, _ffn_chunk(s))),
            pl.BlockSpec((None, None, D, FF_CHUNK), lambda s: (l, k, 0, N_FF_CHUNKS + _ffn_chunk(s))),
            pl.BlockSpec((None, None, FF_CHUNK, D), lambda s: (l, k, _ffn_chunk(s), 0)),
            _layer_spec((3, D), l), _layer_spec((3, D), l)]


_FFN_SCRATCH = [pltpu.VMEM((N_FF_CHUNKS, D, 2 * FF_CHUNK), BF16), pltpu.VMEM((N_FF_CHUNKS, FF_CHUNK, D), BF16)]


def _ffn(l, x_ctx, x_lat, mod, w_up, w_down, ln_g, ln_b):
    return pl.pallas_call(
        _ffn_kernel,
        grid=(N_FFN_STEPS,),
        in_specs=[_ffn_ctx_spec(), _ffn_lat_spec(), _mod_spec(l, _ffn_tile)] + _ffn_weight_specs(l, 0),
        out_specs=_ffn_row_spec(D),
        out_shape=jax.ShapeDtypeStruct((N_TOK, D), F32),
        scratch_shapes=_FFN_SCRATCH,
        compiler_params=_params(1),
        name="ffn",
    )(x_ctx, x_lat, mod, w_up, w_up, w_down, ln_g, ln_b)


def _mix_ffn(l, x, att, lru, w_o, mod, w_up, w_down, ln_g, ln_b):
    return pl.pallas_call(
        _mix_ffn_kernel,
        grid=(N_FFN_STEPS,),
        in_specs=[_ffn_row_spec(D), _ffn_row_spec(MLA_W), _ffn_row_spec(LRU_W), _layer_spec((D, D), l),
                  _mod_spec(l, _ffn_tile)] + _ffn_weight_specs(l, 1),
        out_specs=[_ffn_ctx_spec(), _ffn_lat_spec()],
        out_shape=[jax.ShapeDtypeStruct((N_CTX, D), F32), jax.ShapeDtypeStruct((N_LAT, D), F32)],
        scratch_shapes=_FFN_SCRATCH + [pltpu.VMEM((D, D), BF16), pltpu.VMEM((TM, D), F32)],
        compiler_params=_params(1),
        name="mix_ffn",
    )(x, att, lru, w_o, mod, w_up, w_up, w_down, ln_g, ln_b)


def _rope(v, tab_ref):
    return (v * tab_ref[0, 0] + pltpu.roll(v, ROPE_HALF, axis=1) * tab_ref[0, 1]
            + pltpu.roll(v, HEAD_LANES - ROPE_HALF, axis=1) * tab_ref[0, 2])


def _proj_kernel(x_ref, mod_ref, tab_ref, win_ref, qg_ref, kvg_ref, wuq_ref,
                 q_ref, ckv_ref, kr_ref, ux_ref, gg_ref):
    m = mod_ref[0]
    h = (x_ref[...] * (1.0 + m[4:5]) + m[3:4]).astype(BF16)
    p = _dot(h, win_ref[...])
    o_kv, o_ux, o_ug, o_kr = Q_LORA, Q_LORA + KV_LORA, Q_LORA + KV_LORA + LRU_W, IN_P - HEAD_LANES
    cq = _rms_norm(p[:, :o_kv], qg_ref[...]).astype(BF16)
    q = _dot(cq, wuq_ref[...])
    for hd in range(H):
        sl = slice(hd * HEAD_LANES, (hd + 1) * HEAD_LANES)
        q_ref[:, sl] = (_rope(q[:, sl], tab_ref) * ATTN_SCALE).astype(BF16)
    ckv_ref[...] = _rms_norm(p[:, o_kv:o_ux], kvg_ref[...])
    kr_ref[...] = _rope(p[:, o_kr:], tab_ref)
    ux_ref[...] = p[:, o_ux:o_ug]
    gg_ref[...] = jax.nn.gelu(p[:, o_ug:o_kr]).astype(BF16)


def _proj(l, x, mod, tabs, win, qg, kvg, wuq):
    n_ctx = N_CTX // TM
    per_seq = DEC_SEQ // TM

    def tab_map(i):
        lat = i >= n_ctx
        return (jnp.where(lat, 1, 0), 0, jnp.where(lat, (i - n_ctx) % per_seq, 0), 0)

    outs = [(D, BF16), (KV_LORA, F32), (HEAD_LANES, F32), (LRU_W, F32), (LRU_W, BF16)]
    return pl.pallas_call(
        _proj_kernel,
        grid=(N_TOK // TM,),
        in_specs=[_row_spec(D), _mod_spec(l),
                  pl.BlockSpec((1, 3, TM, HEAD_LANES), tab_map),
                  _layer_spec((D, IN_P), l), _layer_spec((1, Q_LORA), l), _layer_spec((1, KV_LORA), l),
                  _layer_spec((Q_LORA, H * HEAD_LANES), l)],
        out_specs=[_row_spec(w) for w, _ in outs],
        out_shape=[jax.ShapeDtypeStruct((N_TOK, w), dt) for w, dt in outs],
        compiler_params=_params(1),
        name="proj",
    )(x, mod, tabs, win, qg, kvg, wuq)


LANE_SLABS = LRU_W // 128
SEG_PAD = 8
PAD_ROWS = (SEQ_TILE // SEQ) * SEG * (SEQ // SEG + SEG_PAD)


def _segment_blocks(seq_len):
    seg_len = seq_len // SEG
    return [(r0 + s * seg_len, (r0 // seg_len + s) * (seg_len + SEG_PAD))
            for r0 in range(0, SEQ_TILE, seq_len) for s in range(SEG)]


def _permute_in(ux_ref, pad_scr, xp_scr, seq_len):
    seg_len = seq_len // SEG
    pitch = seg_len + SEG_PAD
    for row, base in _segment_blocks(seq_len):
        for c in range(LANE_SLABS):
            pad_scr[c, base:base + seg_len, :] = ux_ref[row:row + seg_len, c * 128:(c + 1) * 128]
    for r0 in range(0, SEQ_TILE, seq_len):
        def gather(j, carry):
            src = pl.ds(r0 // seg_len * pitch + j, SEG, stride=pitch)
            pieces = [pad_scr[c, src, :] for c in range(LANE_SLABS)]
            xp_scr[pl.ds(pl.multiple_of(r0 + j * SEG, SEG), SEG), :] = jnp.concatenate(pieces, axis=1)
            return carry

        lax.fori_loop(0, seg_len, gather, 0, unroll=8)


def _permute_out(h_scr, gg_ref, lru_ref, pad_scr, seq_len):
    seg_len = seq_len // SEG
    pitch = seg_len + SEG_PAD
    for r0 in range(0, SEQ_TILE, seq_len):
        def scatter(j, carry):
            rows = pl.ds(pl.multiple_of(r0 + j * SEG, SEG), SEG)
            hs = h_scr[0, rows] + h_scr[1, rows]
            dst = pl.ds(r0 // seg_len * pitch + j, SEG, stride=pitch)
            for c in range(LANE_SLABS):
                pad_scr[c, dst, :] = hs[:, c * 128:(c + 1) * 128]
            return carry

        lax.fori_loop(0, seg_len, scatter, 0, unroll=8)
    for row, base in _segment_blocks(seq_len):
        for c in range(LANE_SLABS):
            cs = slice(c * 128, (c + 1) * 128)
            g = gg_ref[row:row + seg_len, cs].astype(F32)
            lru_ref[row:row + seg_len, cs] = (pad_scr[c, base:base + seg_len, :] * g).astype(BF16)


def _conv_tile(xp_scr, cw_ref, cb_ref, xc_scr, seq_len):
    seg_len = seq_len // SEG
    seg = lax.broadcasted_iota(jnp.int32, (SEG, LRU_W), 0)
    cw = cw_ref[...]
    for r0 in range(0, SEQ_TILE, seq_len):
        x3 = xp_scr[r0:r0 + seq_len].reshape(seg_len, SEG, LRU_W)
        prev_tail = [jnp.where(seg == 0, 0.0, pltpu.roll(x3[seg_len - k], 1, axis=0)) for k in (2, 1)]
        next_head = jnp.where(seg == SEG - 1, 0.0, pltpu.roll(x3[0], SEG - 1, axis=0))
        xm2 = jnp.concatenate([prev_tail[0][None], prev_tail[1][None], x3[:-2]], axis=0)
        xm1 = jnp.concatenate([prev_tail[1][None], x3[:-1]], axis=0)
        xp1 = jnp.concatenate([x3[1:], next_head[None]], axis=0)
        xc = xm2 * cw[0:1] + xm1 * cw[1:2] + x3 * cw[2:3] + xp1 * cw[3:4] + cb_ref[...]
        xc_scr[r0:r0 + seq_len] = xc.reshape(seq_len, LRU_W)


def _scan_tile(h0, a_scr, u_scr, h_scr, seq_len):
    seg_len = seq_len // SEG
    seg = lax.broadcasted_iota(jnp.int32, (SEG, LRU_W), 0)
    zeros = jnp.zeros((SEG, LRU_W), F32)
    ones = jnp.ones((SEG, LRU_W), F32)
    for r0 in range(0, SEQ_TILE, seq_len):
        def rows(j):
            return pl.ds(pl.multiple_of(r0 + j * SEG, SEG), SEG)

        def pass1(j, carry):
            hf, pf, hb, pb = carry
            rf, rb = rows(j), rows(seg_len - 1 - j)
            af, ab = a_scr[0, rf], a_scr[1, rb]
            return (af * hf + u_scr[0, rf], pf * af, ab * hb + u_scr[1, rb], pb * ab)

        hf_end, pf_end, hb_end, pb_end = lax.fori_loop(0, seg_len, pass1, (zeros, ones, zeros, ones),
                                                       unroll=4)

        cin_f = jnp.where(seg == 0, h0[0:1], 0.0)
        for s in range(1, SEG):
            nxt = pltpu.roll(pf_end * cin_f + hf_end, 1, axis=0)
            cin_f = jnp.where(seg == s, nxt, cin_f)
        cin_b = jnp.where(seg == SEG - 1, h0[1:2], 0.0)
        for s in range(SEG - 2, -1, -1):
            nxt = pltpu.roll(pb_end * cin_b + hb_end, SEG - 1, axis=0)
            cin_b = jnp.where(seg == s, nxt, cin_b)

        def pass2(j, carry):
            hf, hb = carry
            rf, rb = rows(j), rows(seg_len - 1 - j)
            hf = a_scr[0, rf] * hf + u_scr[0, rf]
            hb = a_scr[1, rb] * hb + u_scr[1, rb]
            h_scr[0, rf] = hf
            h_scr[1, rb] = hb
            return hf, hb

        lax.fori_loop(0, seg_len, pass2, (cin_f, cin_b), unroll=4)


def _scan_kernel(ux_ref, gg_ref, h0_ref, cw_ref, cb_ref, wg_ref, ba_ref, bx_ref, lam_ref,
                 lru_ref, st_ref, pad_scr, xp_scr, xc_scr, a_scr, u_scr, h_scr):
    is_ctx = pl.program_id(0) < N_CTX_TILES

    @pl.when(is_ctx)
    def _():
        _permute_in(ux_ref, pad_scr, xp_scr, SEQ)
        _conv_tile(xp_scr, cw_ref, cb_ref, xc_scr, SEQ)

    @pl.when(jnp.logical_not(is_ctx))
    def _():
        _permute_in(ux_ref, pad_scr, xp_scr, DEC_SEQ)
        _conv_tile(xp_scr, cw_ref, cb_ref, xc_scr, DEC_SEQ)

    lam = lam_ref[...]
    softplus_neg_lam = jnp.maximum(-lam, 0.0) + jnp.log1p(jnp.exp(-jnp.abs(lam)))
    k = (-0.5 * LRU_C) * softplus_neg_lam
    half_ba, half_bx = 0.5 * ba_ref[...], 0.5 * bx_ref[...]
    half = LRU_W // 2
    for hh in range(2):
        cs = slice(hh * half, (hh + 1) * half)
        xc = xc_scr[:, cs]
        half_x = 0.5 * xc
        z = _dot(xc.astype(BF16), wg_ref[hh])
        for d in range(2):
            ta = jnp.tanh(0.5 * z[:, (2 * d) * half:(2 * d + 1) * half] + half_ba[d:d + 1, cs])
            tx = jnp.tanh(0.5 * z[:, (2 * d + 1) * half:(2 * d + 2) * half] + half_bx[d:d + 1, cs])
            a = jnp.exp(k[d:d + 1, cs] + k[d:d + 1, cs] * ta)
            a_scr[d, :, cs] = a
            u_scr[d, :, cs] = jnp.sqrt(1.0 - a * a) * (half_x + half_x * tx)

    @pl.when(is_ctx)
    def _():
        _scan_tile(jnp.zeros((2, LRU_W), F32), a_scr, u_scr, h_scr, SEQ)
        for q in range(CTX_PER_TILE):
            st_ref[0, 0, q * SEG:(q + 1) * SEG] = h_scr[0, (q + 1) * SEQ - SEG:(q + 1) * SEQ]
            st_ref[0, 1, q * SEG:(q + 1) * SEG] = h_scr[1, q * SEQ:q * SEQ + SEG]
        _permute_out(h_scr, gg_ref, lru_ref, pad_scr, SEQ)

    @pl.when(jnp.logical_not(is_ctx))
    def _():
        _scan_tile(h0_ref[0], a_scr, u_scr, h_scr, DEC_SEQ)
        st_ref[...] = jnp.zeros(st_ref.shape, F32)
        _permute_out(h_scr, gg_ref, lru_ref, pad_scr, DEC_SEQ)


def _scan(l, ux, gg, h0, cw, cb, wg, ba, bx, lam):
    seq = functools.partial(_row_spec, tm=SEQ_TILE)
    half = LRU_W // 2
    big = pltpu.VMEM((2, SEQ_TILE, LRU_W), F32)
    return pl.pallas_call(
        _scan_kernel,
        grid=(N_TILES,),
        in_specs=[seq(LRU_W), seq(LRU_W),
                  pl.BlockSpec((None, 1, 2, LRU_W), lambda i: (l, jnp.maximum(i - N_CTX_TILES + 1, 0), 0, 0)),
                  _layer_spec((4, LRU_W), l), _layer_spec((1, LRU_W), l), _layer_spec((2, half, 4 * half), l),
                  _layer_spec((2, LRU_W), l), _layer_spec((2, LRU_W), l), _layer_spec((2, LRU_W), l)],
        out_specs=[seq(LRU_W), pl.BlockSpec((1, 2, CTX_PER_TILE * SEG, LRU_W), lambda i: (i, 0, 0, 0))],
        out_shape=[jax.ShapeDtypeStruct((N_TOK, LRU_W), BF16),
                   jax.ShapeDtypeStruct((N_TILES, 2, CTX_PER_TILE * SEG, LRU_W), F32)],
        scratch_shapes=[pltpu.VMEM((LANE_SLABS, PAD_ROWS, 128), F32), pltpu.VMEM((SEQ_TILE, LRU_W), F32),
                        pltpu.VMEM((SEQ_TILE, LRU_W), F32), big, big, big],
        compiler_params=_params(1),
        name="scan",
    )(ux, gg, h0, cw, cb, wg, ba, bx, lam)


def _decompress(ckv, kr, wukv_ref, k_scr, v_scr, r0):
    n = ckv.shape[0]
    kv = _dot(ckv.astype(BF16), wukv_ref[...])
    for hd in range(H):
        k_scr[hd, r0:r0 + n] = (kv[:, hd * HEAD_LANES:(hd + 1) * HEAD_LANES] + kr).astype(BF16)
        v0 = (H + hd) * HEAD_LANES
        v_scr[hd, r0:r0 + n] = kv[:, v0:v0 + HEAD_LANES].astype(BF16)


def _attend(q, k_scr, v_scr, n_keys):
    outs = []
    for pair in range(H // 2):
        o = None
        for hd in (2 * pair, 2 * pair + 1):
            s = lax.dot_general(q[:, hd * HEAD_LANES:(hd + 1) * HEAD_LANES], k_scr[hd, :n_keys],
                                (((1,), (1,)), ((), ())), preferred_element_type=F32)
            p = jnp.exp(s - jnp.max(s, axis=-1, keepdims=True))
            l = jnp.sum(p, axis=-1, keepdims=True)
            oh = _dot(p.astype(BF16), v_scr[hd, :n_keys]) / l
            o = oh if o is None else o + oh
        outs.append(o)
    return jnp.concatenate(outs, axis=-1)


def _attn_kernel(q_ref, ckv_ref, kr_ref, cckv_ref, ckr_ref, wukv_ref, o_ref, k_scr, v_scr):
    is_ctx = pl.program_id(0) < N_CTX_TILES

    @pl.when(is_ctx)
    def _():
        def one_seq(i, carry):
            rows = pl.ds(pl.multiple_of(i * SEQ, SEQ), SEQ)
            _decompress(ckv_ref[rows], kr_ref[rows], wukv_ref, k_scr, v_scr, 0)
            o_ref[rows] = _attend(q_ref[rows], k_scr, v_scr, SEQ).astype(BF16)
            return carry

        lax.fori_loop(0, CTX_PER_TILE, one_seq, 0)

    @pl.when(jnp.logical_not(is_ctx))
    def _():
        _decompress(cckv_ref[0], ckr_ref[0], wukv_ref, k_scr, v_scr, 0)
        _decompress(ckv_ref[...], kr_ref[...], wukv_ref, k_scr, v_scr, PAST)

        def one_block(i, carry):
            rows = pl.ds(pl.multiple_of(i * Q_BLK, Q_BLK), Q_BLK)
            o_ref[rows] = _attend(q_ref[rows], k_scr, v_scr, KEYS_LAT).astype(BF16)
            return carry

        lax.fori_loop(0, DEC_SEQ // Q_BLK, one_block, 0)


def _attention(l, q, ckv, kr, cache_ckv, cache_kr, wukv):
    seq = functools.partial(_row_spec, tm=SEQ_TILE)

    def cache_spec(w):
        return pl.BlockSpec((1, None, PAST, w), lambda i: (jnp.maximum(i - N_CTX_TILES, 0), l, 0, 0))

    kv_scratch = pltpu.VMEM((H, KEYS_LAT, HEAD_LANES), BF16)
    return pl.pallas_call(
        _attn_kernel,
        grid=(N_TILES,),
        in_specs=[seq(D), seq(KV_LORA), seq(HEAD_LANES), cache_spec(KV_LORA), cache_spec(HEAD_LANES),
                  _layer_spec((KV_LORA, 2 * H * HEAD_LANES), l)],
        out_specs=seq(MLA_W),
        out_shape=jax.ShapeDtypeStruct((N_TOK, MLA_W), BF16),
        scratch_shapes=[kv_scratch, kv_scratch],
        compiler_params=_params(1),
        name="attn",
    )(q, ckv, kr, cache_ckv, cache_kr, wukv)


def _pad_rope_lanes(kr):
    pad = [(0, 0)] * (kr.ndim - 1) + [(NOPE, HEAD_LANES - NOPE - ROPE)]
    return jnp.pad(kr, pad)


def _rope_tables():
    n_freq = ROPE // 4
    inv = ROPE_BASE ** (-jnp.arange(n_freq, dtype=F32) / n_freq)
    t = jnp.arange(DEC_SEQ)
    row = (t // GRID_W).astype(F32)
    col = (t % GRID_W).astype(F32)
    ang = jnp.concatenate([row[:, None] * inv, col[:, None] * inv], axis=-1)
    cos, sin = jnp.cos(ang), jnp.sin(ang)
    ones = jnp.ones((DEC_SEQ, NOPE), F32)
    tail = jnp.ones((DEC_SEQ, HEAD_LANES - NOPE - ROPE), F32)
    z = lambda w: jnp.zeros((DEC_SEQ, w), F32)
    scale = jnp.concatenate([ones, cos, cos, tail], axis=-1)
    from_left = jnp.concatenate([z(NOPE + ROPE_HALF), sin, z(HEAD_LANES - NOPE - ROPE)], axis=-1)
    from_right = jnp.concatenate([z(NOPE), -sin, z(HEAD_LANES - NOPE - ROPE_HALF)], axis=-1)
    lat = jnp.stack([scale, from_left, from_right])
    ctx = jnp.stack([jnp.ones_like(scale), jnp.zeros_like(scale), jnp.zeros_like(scale)])
    return jnp.stack([ctx, lat])


def _block_diag_gates(w_a, w_x):
    per_half = LRU_BLOCKS // 2
    eye = jnp.eye(per_half, dtype=w_a.dtype)

    def bd(w):
        w = w.reshape(DEPTH, 2, 2, per_half, LRU_BD, LRU_BD)
        full = jnp.einsum("ldhnij,nm->ldhnimj", w, eye)
        return full.reshape(DEPTH, 2, 2, per_half * LRU_BD, per_half * LRU_BD)

    a, x = bd(w_a), bd(w_x)
    cols = [a[:, 0], x[:, 0], a[:, 1], x[:, 1]]
    return jnp.concatenate(cols, axis=-1).astype(BF16)


def kernel(x_prompt, x_sample, cache_ckv, cache_krope, state_lru, c, c_ctx, w_mod, b_mod, ln_g, ln_b,
           w_ffn_up, w_ffn_down, w_in, q_norm_g, kv_norm_g, w_uq, w_ukv, conv_w, conv_b, lru_w_a,
           lru_b_a, lru_w_x, lru_b_x, lru_lambda, w_o):
    o2, o3 = Q_LORA + KV_LORA, Q_LORA + KV_LORA + ROPE
    win = jnp.concatenate([w_in[..., :o2], w_in[..., o3:], _pad_rope_lanes(w_in[..., o2:o3])],
                          axis=-1).astype(BF16)
    wuq = w_uq.reshape(DEPTH, Q_LORA, H, NOPE + ROPE)
    wuq = jnp.pad(wuq, ((0, 0), (0, 0), (0, 0), (0, HEAD_LANES - NOPE - ROPE)))
    wuq = wuq.reshape(DEPTH, Q_LORA, H * HEAD_LANES).astype(BF16)
    wkv = w_ukv.reshape(DEPTH, KV_LORA, H, NOPE + VH)
    wk = jnp.pad(wkv[..., :NOPE], ((0, 0), (0, 0), (0, 0), (0, HEAD_LANES - NOPE)))
    v = wkv[..., NOPE:]
    odd = (jnp.arange(H) % 2 == 1)[None, None, :, None]
    wv = jnp.concatenate([jnp.where(odd, 0.0, v), jnp.where(odd, v, 0.0)], axis=-1)
    wukv = jnp.concatenate([wk.reshape(DEPTH, KV_LORA, H * HEAD_LANES),
                            wv.reshape(DEPTH, KV_LORA, H * HEAD_LANES)], axis=-1).astype(BF16)
    wg = _block_diag_gates(lru_w_a, lru_w_x)
    cache_kr = _pad_rope_lanes(cache_krope)
    tabs = _rope_tables()
    conds = jnp.concatenate([c_ctx[None], c, jnp.zeros((MOD_ROWS - 1 - DEC_BATCH, D), F32)], axis=0)
    h0 = jnp.concatenate([jnp.zeros((DEPTH, 1, 2, LRU_W), F32), state_lru.transpose(1, 0, 2, 3)], axis=1)
    qg, kvg, cb = q_norm_g[:, None], kv_norm_g[:, None], conv_b[:, None]

    mod = _modulation(conds, w_mod, b_mod).reshape(DEPTH, MOD_ROWS, N_MOD, D)
    x_ctx, x_lat = x_prompt.reshape(N_CTX, D), x_sample.reshape(N_LAT, D)

    ckv_out, kr_out, st_out = [], [], []
    for l in range(DEPTH):
        x = _ffn(l, x_ctx, x_lat, mod, w_ffn_up, w_ffn_down, ln_g, ln_b)
        q, ckv, kr, ux, gg = _proj(l, x, mod, tabs, win, qg, kvg, wuq)
        lru, st = _scan(l, ux, gg, h0, conv_w, cb, wg, lru_b_a, lru_b_x, lru_lambda)
        att = _attention(l, q, ckv, kr, cache_ckv, cache_kr, wukv)
        x_ctx, x_lat = _mix_ffn(l, x, att, lru, w_o, mod, w_ffn_up, w_ffn_down, ln_g, ln_b)
        ckv_out.append(ckv[:N_CTX].reshape(BATCH, SEQ, KV_LORA))
        kr_out.append(kr[:N_CTX, NOPE:NOPE + ROPE].reshape(BATCH, SEQ, ROPE))
        st_out.append(st[:N_CTX_TILES])

    y_prompt = x_ctx.reshape(BATCH, SEQ, D)
    y_sample = x_lat.reshape(DEC_BATCH, DEC_SEQ, D)
    new_ckv = jnp.stack(ckv_out, axis=1)
    new_kr = jnp.stack(kr_out, axis=1)
    st = jnp.stack(st_out, axis=0)
    fwd = st[:, :, 0, SEG - 1::SEG].reshape(DEPTH, BATCH, LRU_W)
    bwd = st[:, :, 1, 0::SEG].reshape(DEPTH, BATCH, LRU_W)
    new_state = jnp.stack([fwd, bwd], axis=2).transpose(1, 0, 2, 3)
    return (y_prompt, y_sample, new_ckv, new_kr, new_state)
```
